```python
import jax, jax.numpy as jnp
from jax import lax
import numpy as np

D_MODEL = 1024
BATCH = 2
SEQ = 16384
DEPTH = 4
DEC_BATCH = 8
DEC_SEQ = 2048
PAST_LEN = 128

N_MIXERS = 2
N_SG_LAYERS = (DEPTH + 1) // 2
N_TM_LAYERS = DEPTH // 2
CHUNK = 128
SG_HALF = 2 * D_MODEL
SG_GROUPS = 8
SG_GROUP_DIM = SG_HALF // SG_GROUPS
HEAD_SIZE = 64
N_HEADS = D_MODEL // HEAD_SIZE
DECAY_LORA = 64
AAA_LORA = 64
GATE_LORA = 160
N_SHIFT_MIX = 6
GN_EPS = 64e-5
D_FF = ((8 * D_MODEL // 3 + 255) // 256) * 256
CONV_WIDTH = 3
RMS_EPS = 1e-6
L2_EPS = 1e-12

kernel_name = 'hybrid_sgu_rwkv7_bidir_encoder'


def rms_norm(x, g):
    xf = x.astype(jnp.float32)
    y = xf * lax.rsqrt(jnp.mean(xf * xf, axis=-1, keepdims=True) + RMS_EPS)
    return (y * g.astype(jnp.float32)).astype(x.dtype)


def spatial_gating_mixer(x, w_in, b_in, norm_g, w_s, b_s, w_out):
    B, S, _ = x.shape
    z = jax.nn.gelu(x @ w_in + b_in, approximate=False)
    u, v = jnp.split(z, 2, axis=-1)
    v = rms_norm(v, norm_g)
    v = v.reshape(B, S // CHUNK, CHUNK, SG_GROUPS, SG_GROUP_DIM)
    v = jnp.einsum('gqp,bcpgd->bcqgd', w_s, v) + b_s.T[None, None, :, :, None]
    v = v.reshape(B, S, SG_HALF)
    return (u * v) @ w_out


def centred_shift_delta(x):
    prev = jnp.pad(x[:, :-1], ((0, 0), (1, 0), (0, 0)))
    nxt = jnp.pad(x[:, 1:], ((0, 0), (0, 1), (0, 0)))
    return 0.5 * (prev + nxt) - x


def wkv_scan(r, w, k, v, kk, a, reverse):
    B, S, H, N = r.shape
    xs = tuple(jnp.moveaxis(t, 1, 0) for t in (r, w, k, v, kk, a))

    def step(state, inp):
        r_t, w_t, k_t, v_t, kk_t, a_t = inp
        sa = jnp.einsum('bhij,bhj->bhi', state, -kk_t)
        state = (state * w_t[:, :, None, :]
                 + sa[..., None] * (kk_t * a_t)[:, :, None, :]
                 + v_t[..., None] * k_t[:, :, None, :])
        y_t = jnp.einsum('bhij,bhj->bhi', state, r_t)
        return state, y_t

    init = jnp.zeros((B, H, N, N), jnp.float32)
    _, y = lax.scan(step, init, xs, reverse=reverse)
    return jnp.moveaxis(y, 0, 1)


def heads(t):
    return t.reshape(t.shape[:-1] + (N_HEADS, HEAD_SIZE))


def rwkv7_bidir_mixer(x, mu, w_r, w_k, w_v, w0, w1, w2, a0, a1, a2, g1, g2,
                      k_k, k_a, r_k, ln_g, ln_b, w_o):
    B, S, D = x.shape
    f32 = jnp.float32
    xx = centred_shift_delta(x)
    xr, xw, xk, xv, xa, xg = (x + xx * mu[n] for n in range(N_SHIFT_MIX))
    r = xr @ w_r
    k = xk @ w_k
    v = xv @ w_v
    g = jax.nn.sigmoid(xg @ g1) @ g2
    w_pre = jnp.einsum('ebsr,erd->ebsd', jnp.tanh(jnp.einsum('bsd,edr->ebsr', xw, w1)), w2) + w0[:, None, None, :]
    w_log = -jax.nn.softplus(-w_pre.astype(f32)) - 0.5
    decay = jnp.exp(-jnp.exp(w_log))
    a_pre = jnp.einsum('ebsr,erd->ebsd', jnp.einsum('bsd,edr->ebsr', xa, a1), a2) + a0[:, None, None, :]
    a = jax.nn.sigmoid(a_pre.astype(f32))
    kf = k.astype(f32)
    kd = kf[None] * (1.0 + (a - 1.0) * k_a.astype(f32))
    kk = heads(kf * k_k.astype(f32))
    kk = kk / jnp.maximum(jnp.sqrt(jnp.sum(kk * kk, axis=-1, keepdims=True)), L2_EPS)
    rh = heads(r.astype(f32))
    vh = heads(v.astype(f32))
    kdh = heads(kd)
    ah = heads(a)
    dh = heads(decay)
    y = (wkv_scan(rh, dh[0], kdh[0], vh, kk, ah[0], False)
         + wkv_scan(rh, dh[1], kdh[1], vh, kk, ah[1], True))
    mean = jnp.mean(y, axis=-1, keepdims=True)
    var = jnp.mean(jnp.square(y - mean), axis=-1, keepdims=True)
    yn = ((y - mean) * lax.rsqrt(var + GN_EPS)).reshape(B, S, D) * ln_g.astype(f32) + ln_b.astype(f32)
    bonus = jnp.sum(rh[None] * kdh * r_k.astype(f32), axis=(0, -1))[..., None] * vh
    out = (yn + bonus.reshape(B, S, D)).astype(x.dtype) * g
    return out @ w_o


def conv_glu_ffn(x, w_gate, w_up, conv_w, conv_b, w_down):
    S = x.shape[1]
    gate = x @ w_gate
    up = x @ w_up
    half = CONV_WIDTH // 2
    padded = jnp.pad(gate, ((0, 0), (half, CONV_WIDTH - 1 - half), (0, 0)))
    conv = sum(padded[:, t:t + S] * conv_w[t] for t in range(CONV_WIDTH)) + conv_b
    h = jax.nn.gelu(conv, approximate=False) * up
    return h @ w_down


def trunk(x, norm_mix_g, norm_ffn_g, final_norm_g,
          sg_w_in, sg_b_in, sg_norm_g, sg_w_s, sg_b_s, sg_w_out,
          tm_mu, tm_w_r, tm_w_k, tm_w_v, tm_w0, tm_w1, tm_w2, tm_a0, tm_a1, tm_a2,
          tm_g1, tm_g2, tm_k_k, tm_k_a, tm_r_k, tm_ln_g, tm_ln_b, tm_w_o,
          ff_w_gate, ff_w_up, ff_conv_w, ff_conv_b, ff_w_down):
    for i in range(DEPTH):
        h = rms_norm(x, norm_mix_g[i])
        j = i // N_MIXERS
        if i % N_MIXERS == 0:
            x = x + spatial_gating_mixer(h, sg_w_in[j], sg_b_in[j], sg_norm_g[j],
                                         sg_w_s[j], sg_b_s[j], sg_w_out[j])
        else:
            x = x + rwkv7_bidir_mixer(h, tm_mu[j], tm_w_r[j], tm_w_k[j], tm_w_v[j],
                                      tm_w0[j], tm_w1[j], tm_w2[j], tm_a0[j], tm_a1[j], tm_a2[j],
                                      tm_g1[j], tm_g2[j], tm_k_k[j], tm_k_a[j], tm_r_k[j],
                                      tm_ln_g[j], tm_ln_b[j], tm_w_o[j])
        h = rms_norm(x, norm_ffn_g[i])
        x = x + conv_glu_ffn(h, ff_w_gate[i], ff_w_up[i], ff_conv_w[i], ff_conv_b[i], ff_w_down[i])
    return rms_norm(x, final_norm_g)


def setup_inputs(seed: int = 0) -> dict:
    key = jax.random.key(seed)
    ks = iter(jax.random.split(key, 64))

    def nrm(shape, scale):
        return scale * jax.random.normal(next(ks), shape, jnp.float32)

    def gain(shape):
        return 1.0 + nrm(shape, 0.02)

    D, E, F = D_MODEL, SG_HALF, D_FF
    NA, NB = N_SG_LAYERS, N_TM_LAYERS
    decay_base = jnp.linspace(-6.0, -1.0, D, dtype=jnp.float32)
    return {
        'x_prompt': nrm((BATCH, SEQ, D), 1.0),
        'x_sample': nrm((DEC_BATCH, DEC_SEQ, D), 1.0),
        'norm_mix_g': gain((DEPTH, D)),
        'norm_ffn_g': gain((DEPTH, D)),
        'final_norm_g': gain((D,)),
        'sg_w_in': nrm((NA, D, 2 * E), D ** -0.5),
        'sg_b_in': nrm((NA, 2 * E), 0.02),
        'sg_norm_g': gain((NA, E)),
        'sg_w_s': nrm((NA, SG_GROUPS, CHUNK, CHUNK), CHUNK ** -0.5),
        'sg_b_s': gain((NA, SG_GROUPS, CHUNK)),
        'sg_w_out': nrm((NA, E, D), E ** -0.5),
        'tm_mu': jax.random.uniform(next(ks), (NB, N_SHIFT_MIX, D), jnp.float32),
        'tm_w_r': nrm((NB, D, D), D ** -0.5),
        'tm_w_k': nrm((NB, D, D), D ** -0.5),
        'tm_w_v': nrm((NB, D, D), D ** -0.5),
        'tm_w0': decay_base + nrm((NB, 2, D), 0.1),
        'tm_w1': nrm((NB, 2, D, DECAY_LORA), D ** -0.5),
        'tm_w2': nrm((NB, 2, DECAY_LORA, D), 0.1 * DECAY_LORA ** -0.5),
        'tm_a0': nrm((NB, 2, D), 0.1),
        'tm_a1': nrm((NB, 2, D, AAA_LORA), D ** -0.5),
        'tm_a2': nrm((NB, 2, AAA_LORA, D), 0.5 * AAA_LORA ** -0.5),
        'tm_g1': nrm((NB, D, GATE_LORA), D ** -0.5),
        'tm_g2': nrm((NB, GATE_LORA, D), GATE_LORA ** -0.5),
        'tm_k_k': 0.85 + nrm((NB, D), 0.02),
        'tm_k_a': gain((NB, D)),
        'tm_r_k': nrm((NB, N_HEADS, HEAD_SIZE), 0.1),
        'tm_ln_g': gain((NB, D)),
        'tm_ln_b': nrm((NB, D), 0.02),
        'tm_w_o': nrm((NB, D, D), D ** -0.5),
        'ff_w_gate': nrm((DEPTH, D, F), D ** -0.5),
        'ff_w_up': nrm((DEPTH, D, F), D ** -0.5),
        'ff_conv_w': nrm((DEPTH, CONV_WIDTH, F), CONV_WIDTH ** -0.5),
        'ff_conv_b': nrm((DEPTH, F), 0.02),
        'ff_w_down': nrm((DEPTH, F, D), F ** -0.5),
    }


def reference(x_prompt, x_sample, norm_mix_g, norm_ffn_g, final_norm_g,
              sg_w_in, sg_b_in, sg_norm_g, sg_w_s, sg_b_s, sg_w_out,
              tm_mu, tm_w_r, tm_w_k, tm_w_v, tm_w0, tm_w1, tm_w2, tm_a0, tm_a1, tm_a2,
              tm_g1, tm_g2, tm_k_k, tm_k_a, tm_r_k, tm_ln_g, tm_ln_b, tm_w_o,
              ff_w_gate, ff_w_up, ff_conv_w, ff_conv_b, ff_w_down):
    params = (norm_mix_g, norm_ffn_g, final_norm_g,
              sg_w_in, sg_b_in, sg_norm_g, sg_w_s, sg_b_s, sg_w_out,
              tm_mu, tm_w_r, tm_w_k, tm_w_v, tm_w0, tm_w1, tm_w2, tm_a0, tm_a1, tm_a2,
              tm_g1, tm_g2, tm_k_k, tm_k_a, tm_r_k, tm_ln_g, tm_ln_b, tm_w_o,
              ff_w_gate, ff_w_up, ff_conv_w, ff_conv_b, ff_w_down)
    y_prompt = trunk(x_prompt, *params)
    y_sample = trunk(x_sample, *params)
    return (y_prompt, y_sample)
```

```python
import functools
import math

import jax
import jax.numpy as jnp
from jax import lax
from jax.experimental import pallas as pl
from jax.experimental.pallas import tpu as pltpu

F32 = jnp.float32
BF16 = jnp.bfloat16

D_MODEL = 1024
SG_CHUNK = 128
SG_HALF = 2 * D_MODEL
SG_GROUPS = 8
SG_GROUP_DIM = SG_HALF // SG_GROUPS
HEAD = 64
PAIR = 2 * HEAD
N_PAIRS = D_MODEL // PAIR
LORA_W = 64
LORA_A = 64
GN_EPS = 64e-5
RMS_EPS = 1e-6
L2_EPS = 1e-12
HALO = 8
SCAN_L = 64
VMEM_LIMIT = 56 * 1024 * 1024

TM_SG = 256
TM_FFN = 256
TM_TM = 256
SCAN_ROWS = 256


def _rms(x, g):
    return x * lax.rsqrt(jnp.mean(x * x, axis=-1, keepdims=True) + RMS_EPS) * g


def _gelu(x):
    return 0.5 * x * (1.0 + lax.erf(x * (1.0 / math.sqrt(2.0))))


def _dot(a, b):
    return jnp.dot(a, b, preferred_element_type=F32)


def _dot_nt(a, b):
    return lax.dot_general(a, b, (((1,), (1,)), ((), ())), preferred_element_type=F32)


def _dot_tn(a, b):
    return lax.dot_general(a, b, (((0,), (0,)), ((), ())), preferred_element_type=F32)


def _dot_hilo(a, ones_bf16):
    hi = a.astype(BF16)
    lo = (a - hi.astype(F32)).astype(BF16)
    return _dot(hi, ones_bf16) + _dot(lo, ones_bf16)


def _const_spec(shape):
    nd = len(shape)
    return pl.BlockSpec(shape, lambda *_: (0,) * nd, pipeline_mode=pl.Buffered(1))


def _params(n_axes=1):
    return pltpu.CompilerParams(dimension_semantics=("arbitrary",) * n_axes,
                                vmem_limit_bytes=VMEM_LIMIT)


def _halo_specs(tm, total_rows):
    per = tm // HALO
    last = total_rows // HALO - 1
    prev = pl.BlockSpec((HALO, D_MODEL), lambda i: (jnp.maximum(i * per - 1, 0), 0))
    nxt = pl.BlockSpec((HALO, D_MODEL), lambda i: (jnp.minimum((i + 1) * per, last), 0))
    return prev, nxt


def _edge_keep_mask(tm, tiles_per_seq):
    i = pl.program_id(0)
    pos = i % tiles_per_seq
    row = lax.broadcasted_iota(jnp.int32, (tm + 2 * HALO, 1), 0)
    drop = jnp.logical_or(jnp.logical_and(pos == 0, row < HALO),
                          jnp.logical_and(pos == tiles_per_seq - 1, row >= tm + HALO))
    return jnp.logical_not(drop)


def _neighbours(ext, tm):
    n = tm + 2 * HALO
    prev = pltpu.roll(ext, 1, axis=0)[HALO:HALO + tm]
    nxt = pltpu.roll(ext, n - 1, axis=0)[HALO:HALO + tm]
    return prev, ext[HALO:HALO + tm], nxt


def _sg_kernel(x_ref, g_ref, win_ref, bin_ref, ng_ref, ws_ref, bs_ref, wout_ref, o_ref, uv_ref, *, tm):
    x = x_ref[...]
    h = _rms(x, g_ref[...]).astype(BF16)
    z = _gelu(_dot(h, win_ref[...]) + bin_ref[...])
    u = z[:, :SG_HALF]
    v = _rms(z[:, SG_HALF:], ng_ref[...]).astype(BF16)
    for c in range(tm // SG_CHUNK):
        rows = slice(c * SG_CHUNK, (c + 1) * SG_CHUNK)
        for g in range(SG_GROUPS):
            cols = slice(g * SG_GROUP_DIM, (g + 1) * SG_GROUP_DIM)
            mixed = _dot(ws_ref[g], v[rows, cols]) + bs_ref[:, cols]
            uv_ref[rows, cols] = (u[rows, cols] * mixed).astype(BF16)
    o_ref[...] = x + _dot(uv_ref[...], wout_ref[...])


def _sg_layer(x, g, w_in, b_in, norm_g, w_s, b_full, w_out):
    rows = x.shape[0]
    tm = TM_SG
    row_spec = pl.BlockSpec((tm, D_MODEL), lambda i: (i, 0))
    return pl.pallas_call(
        functools.partial(_sg_kernel, tm=tm),
        grid=(rows // tm,),
        in_specs=[row_spec, _const_spec(g.shape), _const_spec(w_in.shape), _const_spec(b_in.shape),
                  _const_spec(norm_g.shape), _const_spec(w_s.shape), _const_spec(b_full.shape),
                  _const_spec(w_out.shape)],
        out_specs=row_spec,
        out_shape=jax.ShapeDtypeStruct(x.shape, F32),
        scratch_shapes=[pltpu.VMEM((tm, SG_HALF), BF16)],
        compiler_params=_params(),
        name="sg_mixer",
    )(x, g, w_in, b_in, norm_g, w_s, b_full, w_out)


def _ffn_kernel(xp_ref, x_ref, xn_ref, g_ref, wg_ref, wu_ref, cw_ref, cb_ref, wd_ref, fg_ref, o_ref,
                *, tm, tiles_per_seq, final):
    x = x_ref[...]
    xe = jnp.concatenate([xp_ref[...], x, xn_ref[...]], axis=0)
    he = _rms(xe, g_ref[...]).astype(BF16)
    gate = jnp.where(_edge_keep_mask(tm, tiles_per_seq), _dot(he, wg_ref[...]), 0.0)
    g_prev, g_mid, g_next = _neighbours(gate, tm)
    conv = g_prev * cw_ref[0:1, :] + g_mid * cw_ref[1:2, :] + g_next * cw_ref[2:3, :] + cb_ref[...]
    up = _dot(he[HALO:HALO + tm], wu_ref[...])
    hh = (_gelu(conv) * up).astype(BF16)
    y = x + _dot(hh, wd_ref[...])
    if final:
        y = _rms(y, fg_ref[...])
    o_ref[...] = y


def _ffn_layer(x, seq_len, g, w_gate, w_up, conv_w, conv_b, w_down, final_g, final):
    rows = x.shape[0]
    tm = TM_FFN
    row_spec = pl.BlockSpec((tm, D_MODEL), lambda i: (i, 0))
    prev_spec, next_spec = _halo_specs(tm, rows)
    return pl.pallas_call(
        functools.partial(_ffn_kernel, tm=tm, tiles_per_seq=seq_len // tm, final=final),
        grid=(rows // tm,),
        in_specs=[prev_spec, row_spec, next_spec, _const_spec(g.shape), _const_spec(w_gate.shape),
                  _const_spec(w_up.shape), _const_spec(conv_w.shape), _const_spec(conv_b.shape),
                  _const_spec(w_down.shape), _const_spec(final_g.shape)],
        out_specs=row_spec,
        out_shape=jax.ShapeDtypeStruct(x.shape, F32),
        compiler_params=_params(),
        name="conv_glu_ffn",
    )(x, x, x, g, w_gate, w_up, conv_w, conv_b, w_down, final_g)


def _softplus(z):
    return jnp.maximum(z, 0.0) + jnp.log1p(jnp.exp(-jnp.abs(z)))


def _tm_pre_kernel(xp_ref, x_ref, xn_ref, g_ref, mu_ref, wr_ref, wk_ref, wv_ref, g1_ref, g2_ref,
                   w1_ref, w2_ref, w0_ref, a1_ref, a2_ref, a0_ref, kk_ref, ka_ref, rk_ref, ones_ref,
                   r_o, v_o, kk_o, gate_o, bonus_o, lw0_o, kd0_o, b0_o, lw1_o, kd1_o, b1_o,
                   *, tm, tiles_per_seq):
    xe = jnp.concatenate([xp_ref[...], x_ref[...], xn_ref[...]], axis=0)
    he = jnp.where(_edge_keep_mask(tm, tiles_per_seq), _rms(xe, g_ref[...]), 0.0)
    h_prev, h, h_next = _neighbours(he, tm)
    xx = 0.5 * (h_prev + h_next) - h

    def mix(n):
        return (h + xx * mu_ref[n:n + 1, :]).astype(BF16)

    r = _dot(mix(0), wr_ref[...])
    k = _dot(mix(2), wk_ref[...])
    v = _dot(mix(3), wv_ref[...])
    gate_o[...] = _dot(jax.nn.sigmoid(_dot(mix(5), g1_ref[...])).astype(BF16), g2_ref[...])
    w_lora = jnp.tanh(_dot(mix(1), w1_ref[...])).astype(BF16)
    a_lora = _dot(mix(4), a1_ref[...]).astype(BF16)

    ones = ones_ref[...]
    kk_raw = k * kk_ref[...]
    kk = kk_raw / jnp.maximum(jnp.sqrt(_dot_hilo(kk_raw * kk_raw, ones)), L2_EPS)
    r_o[...] = r
    v_o[...] = v
    kk_o[...] = kk

    kd_sum = None
    for e, (lw_o, kd_o, b_o) in enumerate(((lw0_o, kd0_o, b0_o), (lw1_o, kd1_o, b1_o))):
        w_pre = _dot(w_lora, w2_ref[e]) + w0_ref[e:e + 1, :]
        lw_o[...] = -jnp.exp(-_softplus(-w_pre) - 0.5)
        a = jax.nn.sigmoid(_dot(a_lora, a2_ref[e]) + a0_ref[e:e + 1, :])
        kd = k * (1.0 + (a - 1.0) * ka_ref[...])
        kd_o[...] = kd
        b_o[...] = kk * a
        kd_sum = kd if kd_sum is None else kd_sum + kd
    bonus_o[...] = _dot_hilo(r * kd_sum * rk_ref[...], ones) * v


def _tm_pre(x, seq_len, g, p):
    rows = x.shape[0]
    tm = TM_TM
    row_spec = pl.BlockSpec((tm, D_MODEL), lambda i: (i, 0))
    prev_spec, next_spec = _halo_specs(tm, rows)
    consts = (g, p["mu"], p["w_r"], p["w_k"], p["w_v"], p["g1"], p["g2"], p["w1"], p["w2"], p["w0"],
              p["a1"], p["a2"], p["a0"], p["k_k"], p["k_a"], p["r_k"], p["ones"])
    return pl.pallas_call(
        functools.partial(_tm_pre_kernel, tm=tm, tiles_per_seq=seq_len // tm),
        grid=(rows // tm,),
        in_specs=[prev_spec, row_spec, next_spec] + [_const_spec(c.shape) for c in consts],
        out_specs=[row_spec] * 11,
        out_shape=[jax.ShapeDtypeStruct(x.shape, F32)] * 11,
        compiler_params=_params(),
        name="rwkv7_projections",
    )(x, x, x, *consts)


def _block_diag(x, left):
    zero = jnp.zeros_like(x)
    return jnp.concatenate([jnp.where(left, x, zero), jnp.where(left, zero, x)], axis=0)


def _scan_kernel(r_ref, v_ref, kk_ref, lw_ref, kd_ref, b_ref, y_ref, state_ref, *, rows, reverse):
    L = SCAN_L
    n_chunks = rows // L

    @pl.when(pl.program_id(1) == 0)
    def _():
        state_ref[...] = jnp.zeros_like(state_ref)

    t_idx = lax.broadcasted_iota(jnp.int32, (L, PAIR), 0)
    lane = lax.broadcasted_iota(jnp.int32, (L, PAIR), 1)
    s_idx = lane % L
    left = lane < HEAD
    if reverse:
        strict, incl = s_idx > t_idx, s_idx >= t_idx
    else:
        strict, incl = s_idx < t_idx, s_idx <= t_idx
    tri_t = lax.broadcasted_iota(jnp.int32, (L, L), 0)
    tri_s = lax.broadcasted_iota(jnp.int32, (L, L), 1)
    tri = jnp.where((tri_s >= tri_t) if reverse else (tri_s <= tri_t), 1.0, 0.0).astype(BF16)
    eye = jnp.where(s_idx == t_idx, 1.0, 0.0)
    left_state = lax.broadcasted_iota(jnp.int32, (HEAD, PAIR), 1) < HEAD
    last = 0 if reverse else L - 1

    def chunk(c, carry):
        cc = (n_chunks - 1 - c) if reverse else c
        rs = pl.ds(pl.multiple_of(cc * L, L), L)
        for p in range(N_PAIRS):
            ls = slice(p * PAIR, (p + 1) * PAIR)
            lw = lw_ref[rs, ls]
            lw_hi = lw.astype(BF16)
            rem = lw - lw_hi.astype(F32)
            lw_mid = rem.astype(BF16)
            lw_lo = (rem - lw_mid.astype(F32)).astype(BF16)
            cum = _dot(tri, lw_hi) + _dot(tri, lw_mid) + _dot(tri, lw_lo)
            cum_prev = cum - lw
            total = cum[last:last + 1, :]
            half = 0.5 * total
            r, v, kk, kd, b = r_ref[rs, ls], v_ref[rs, ls], kk_ref[rs, ls], kd_ref[rs, ls], b_ref[rs, ls]
            e_neg = jnp.exp(half - cum)
            e_end = jnp.exp(total - cum)
            q_in = jnp.concatenate([kk * jnp.exp(cum_prev - half), r * jnp.exp(cum - half)], axis=0).astype(BF16)
            k_in = jnp.concatenate([_block_diag((b * e_neg).astype(BF16), left),
                                    _block_diag((kd * e_neg).astype(BF16), left)], axis=0)
            a_all = _dot_nt(q_in, k_in)
            a_ab = jnp.where(strict, a_all[:L, :PAIR], 0.0)
            a_ak = jnp.where(strict, a_all[:L, PAIR:], 0.0)
            a_r = jnp.concatenate([jnp.where(incl, a_all[L:, :PAIR], 0.0),
                                   jnp.where(incl, a_all[L:, PAIR:], 0.0)], axis=1).astype(BF16)
            t_inv = eye - a_ab
            pw = a_ab
            for _ in range(int(math.log2(L)) - 1):
                pw_b = pw.astype(BF16)
                pw = _dot(pw_b, _block_diag(pw_b, left))
                t_inv = t_inv + _dot(t_inv.astype(BF16), _block_diag(pw.astype(BF16), left))
            state = state_ref[p]
            q_st = jnp.concatenate([kk * jnp.exp(cum_prev), r * jnp.exp(cum)], axis=0).astype(BF16)
            from_state = _dot_nt(q_st, _block_diag(state.astype(BF16), left_state))
            v_bd = _block_diag(v.astype(BF16), left)
            rhs = from_state[:L] + _dot(a_ak.astype(BF16), v_bd)
            u = -_dot(t_inv.astype(BF16), _block_diag(rhs.astype(BF16), left))
            u_b = u.astype(BF16)
            y = from_state[L:] + _dot(a_r, jnp.concatenate([_block_diag(u_b, left), v_bd], axis=0))
            y_ref[rs, ls] = y
            upd = _dot_tn(jnp.concatenate([u_b, v.astype(BF16)], axis=0),
                          jnp.concatenate([(b * e_end).astype(BF16), (kd * e_end).astype(BF16)], axis=0))
            state_ref[p] = state * jnp.exp(total) + jnp.where(left_state, upd[:HEAD], upd[HEAD:])
        return carry

    lax.fori_loop(0, n_chunks, chunk, 0)


def _scan(r, v, kk, lw, kd, b, batch, seq_len, reverse):
    rows = SCAN_ROWS
    nblk = seq_len // rows

    def idx(bi, j):
        return (bi * nblk + ((nblk - 1 - j) if reverse else j), 0)

    spec = pl.BlockSpec((rows, D_MODEL), idx)
    return pl.pallas_call(
        functools.partial(_scan_kernel, rows=rows, reverse=reverse),
        grid=(batch, nblk),
        in_specs=[spec] * 6,
        out_specs=spec,
        out_shape=jax.ShapeDtypeStruct(r.shape, F32),
        scratch_shapes=[pltpu.VMEM((N_PAIRS, HEAD, PAIR), F32)],
        compiler_params=_params(2),
        name="rwkv7_scan_bwd" if reverse else "rwkv7_scan_fwd",
    )(r, v, kk, lw, kd, b)


def _tm_post_kernel(x_ref, yf_ref, yb_ref, bonus_ref, gate_ref, lng_ref, lnb_ref, ones_ref, wo_ref, o_ref):
    ones = ones_ref[...]
    y = yf_ref[...] + yb_ref[...]
    d = y - _dot_hilo(y, ones) * (1.0 / HEAD)
    var = _dot_hilo(d * d, ones) * (1.0 / HEAD)
    yn = d * lax.rsqrt(var + GN_EPS) * lng_ref[...] + lnb_ref[...]
    out = ((yn + bonus_ref[...]) * gate_ref[...]).astype(BF16)
    o_ref[...] = x_ref[...] + _dot(out, wo_ref[...])


def _tm_post(x, yf, yb, bonus, gate, p):
    rows = x.shape[0]
    tm = TM_TM
    row_spec = pl.BlockSpec((tm, D_MODEL), lambda i: (i, 0))
    consts = (p["ln_g"], p["ln_b"], p["ones"], p["w_o"])
    return pl.pallas_call(
        _tm_post_kernel,
        grid=(rows // tm,),
        in_specs=[row_spec] * 5 + [_const_spec(c.shape) for c in consts],
        out_specs=row_spec,
        out_shape=jax.ShapeDtypeStruct(x.shape, F32),
        compiler_params=_params(),
        name="rwkv7_output",
    )(x, yf, yb, bonus, gate, *consts)


def _row(v):
    return v.reshape(1, -1).astype(F32)


def _lora_out_padded(w2):
    z = jnp.zeros_like(w2[0])
    return jnp.stack([jnp.concatenate([w2[0], z], axis=0), jnp.concatenate([z, w2[1]], axis=0)]).astype(BF16)


def _prepare(norm_mix_g, norm_ffn_g, final_norm_g,
             sg_w_in, sg_b_in, sg_norm_g, sg_w_s, sg_b_s, sg_w_out,
             tm_mu, tm_w_r, tm_w_k, tm_w_v, tm_w0, tm_w1, tm_w2, tm_a0, tm_a1, tm_a2,
             tm_g1, tm_g2, tm_k_k, tm_k_a, tm_r_k, tm_ln_g, tm_ln_b, tm_w_o,
             ff_w_gate, ff_w_up, ff_conv_w, ff_conv_b, ff_w_down):
    head_id = jnp.arange(D_MODEL) // HEAD
    ones = (head_id[:, None] == head_id[None, :]).astype(BF16)
    sg, tm, ff = [], [], []
    for j in range(sg_w_in.shape[0]):
        sg.append(dict(
            w_in=sg_w_in[j].astype(BF16), b_in=_row(sg_b_in[j]), norm_g=_row(sg_norm_g[j]),
            w_s=sg_w_s[j].astype(BF16),
            b_full=jnp.repeat(sg_b_s[j].T, SG_GROUP_DIM, axis=1).astype(F32),
            w_out=sg_w_out[j].astype(BF16)))
    for j in range(tm_w_r.shape[0]):
        tm.append(dict(
            mu=tm_mu[j], w_r=tm_w_r[j].astype(BF16), w_k=tm_w_k[j].astype(BF16), w_v=tm_w_v[j].astype(BF16),
            g1=tm_g1[j].astype(BF16), g2=tm_g2[j].astype(BF16),
            w1=jnp.concatenate([tm_w1[j, 0], tm_w1[j, 1]], axis=1).astype(BF16), w2=_lora_out_padded(tm_w2[j]),
            w0=tm_w0[j],
            a1=jnp.concatenate([tm_a1[j, 0], tm_a1[j, 1]], axis=1).astype(BF16), a2=_lora_out_padded(tm_a2[j]),
            a0=tm_a0[j],
            k_k=_row(tm_k_k[j]), k_a=_row(tm_k_a[j]), r_k=_row(tm_r_k[j]),
            ln_g=_row(tm_ln_g[j]), ln_b=_row(tm_ln_b[j]), w_o=tm_w_o[j].astype(BF16), ones=ones))
    for i in range(ff_w_gate.shape[0]):
        ff.append(dict(w_gate=ff_w_gate[i].astype(BF16), w_up=ff_w_up[i].astype(BF16), conv_w=ff_conv_w[i],
                       conv_b=_row(ff_conv_b[i]), w_down=ff_w_down[i].astype(BF16)))
    return dict(mix_g=norm_mix_g, ffn_g=norm_ffn_g, final_g=_row(final_norm_g), sg=sg, tm=tm, ff=ff)


def _trunk(x3, prm):
    batch, seq_len, _ = x3.shape
    x = x3.reshape(batch * seq_len, D_MODEL)
    depth = prm["mix_g"].shape[0]
    for i in range(depth):
        g = _row(prm["mix_g"][i])
        if i % 2 == 0:
            p = prm["sg"][i // 2]
            x = _sg_layer(x, g, p["w_in"], p["b_in"], p["norm_g"], p["w_s"], p["b_full"], p["w_out"])
        else:
            p = prm["tm"][i // 2]
            r, v, kk, gate, bonus, lw0, kd0, b0, lw1, kd1, b1 = _tm_pre(x, seq_len, g, p)
            yf = _scan(r, v, kk, lw0, kd0, b0, batch, seq_len, False)
            yb = _scan(r, v, kk, lw1, kd1, b1, batch, seq_len, True)
            x = _tm_post(x, yf, yb, bonus, gate, p)
        f = prm["ff"][i]
        x = _ffn_layer(x, seq_len, _row(prm["ffn_g"][i]), f["w_gate"], f["w_up"], f["conv_w"], f["conv_b"],
                       f["w_down"], prm["final_g"], final=(i == depth - 1))
    return x.reshape(batch, seq_len, D_MODEL)


def kernel(x_prompt, x_sample, norm_mix_g, norm_ffn_g, final_norm_g, sg_w_in, sg_b_in, sg_norm_g, sg_w_s, sg_b_s, sg_w_out, tm_mu, tm_w_r, tm_w_k, tm_w_v, tm_w0, tm_w1, tm_w2, tm_a0, tm_a1, tm_a2, tm_g1, tm_g2, tm_k_k, tm_k_a, tm_r_k, tm_ln_g, tm_ln_b, tm_w_o, ff_w_gate, ff_w_up, ff_conv_w, ff_conv_b, ff_w_down):
    prm = _prepare(norm_mix_g, norm_ffn_g, final_norm_g, sg_w_in, sg_b_in, sg_norm_g, sg_w_s, sg_b_s, sg_w_out,
                   tm_mu, tm_w_r, tm_w_k, tm_w_v, tm_w0, tm_w1, tm_w2, tm_a0, tm_a1, tm_a2,
                   tm_g1, tm_g2, tm_k_k, tm_k_a, tm_r_k, tm_ln_g, tm_ln_b, tm_w_o,
                   ff_w_gate, ff_w_up, ff_conv_w, ff_conv_b, ff_w_down)
    return (_trunk(x_prompt, prm), _trunk(x_sample, prm))
```

```python
import functools
import math

import jax
import jax.numpy as jnp
from jax import lax
from jax.experimental import pallas as pl
from jax.experimental.pallas import tpu as pltpu

F32 = jnp.float32
BF16 = jnp.bfloat16

D_MODEL = 1024
SG_CHUNK = 128
SG_HALF = 2 * D_MODEL
SG_GROUPS = 8
SG_GROUP_DIM = SG_HALF // SG_GROUPS
HEAD = 64
PAIR = 2 * HEAD
N_PAIRS = D_MODEL // PAIR
LORA_W = 64
LORA_A = 64
GN_EPS = 64e-5
RMS_EPS = 1e-6
L2_EPS = 1e-12
HALO = 8
SCAN_L = 64
VMEM_LIMIT = 56 * 1024 * 1024

TM_SG = 256
TM_FFN = 256
TM_TM = 256
SCAN_ROWS = 256


def _rms(x, g):
    return x * lax.rsqrt(jnp.mean(x * x, axis=-1, keepdims=True) + RMS_EPS) * g


def _gelu(x):
    return 0.5 * x * (1.0 + lax.erf(x * (1.0 / math.sqrt(2.0))))


def _dot(a, b):
    return jnp.dot(a, b, preferred_element_type=F32)


def _dot_nt(a, b):
    return lax.dot_general(a, b, (((1,), (1,)), ((), ())), preferred_element_type=F32)


def _dot_tn(a, b):
    return lax.dot_general(a, b, (((0,), (0,)), ((), ())), preferred_element_type=F32)


def _dot_hilo(a, ones_bf16):
    hi = a.astype(BF16)
    lo = (a - hi.astype(F32)).astype(BF16)
    return _dot(hi, ones_bf16) + _dot(lo, ones_bf16)


def _const_spec(shape):
    nd = len(shape)
    return pl.BlockSpec(shape, lambda *_: (0,) * nd, pipeline_mode=pl.Buffered(1))


def _params(n_axes=1):
    return pltpu.CompilerParams(dimension_semantics=("arbitrary",) * n_axes,
                                vmem_limit_bytes=VMEM_LIMIT)


def _halo_specs(tm, total_rows):
    per = tm // HALO
    last = total_rows // HALO - 1
    prev = pl.BlockSpec((HALO, D_MODEL), lambda i: (jnp.maximum(i * per - 1, 0), 0))
    nxt = pl.BlockSpec((HALO, D_MODEL), lambda i: (jnp.minimum((i + 1) * per, last), 0))
    return prev, nxt


def _edge_keep_mask(tm, tiles_per_seq):
    i = pl.program_id(0)
    pos = i % tiles_per_seq
    row = lax.broadcasted_iota(jnp.int32, (tm + 2 * HALO, 1), 0)
    drop = jnp.logical_or(jnp.logical_and(pos == 0, row < HALO),
                          jnp.logical_and(pos == tiles_per_seq - 1, row >= tm + HALO))
    return jnp.logical_not(drop)


def _neighbours(ext, tm):
    n = tm + 2 * HALO
    prev = pltpu.roll(ext, 1, axis=0)[HALO:HALO + tm]
    nxt = pltpu.roll(ext, n - 1, axis=0)[HALO:HALO + tm]
    return prev, ext[HALO:HALO + tm], nxt


def _sg_kernel(x_ref, g_ref, win_ref, bin_ref, ng_ref, ws_ref, bs_ref, wout_ref, o_ref, uv_ref, *, tm):
    x = x_ref[...]
    h = _rms(x, g_ref[...]).astype(BF16)
    z = _gelu(_dot(h, win_ref[...]) + bin_ref[...])
    u = z[:, :SG_HALF]
    v = _rms(z[:, SG_HALF:], ng_ref[...]).astype(BF16)
    for c in range(tm // SG_CHUNK):
        rows = slice(c * SG_CHUNK, (c + 1) * SG_CHUNK)
        for g in range(SG_GROUPS):
            cols = slice(g * SG_GROUP_DIM, (g + 1) * SG_GROUP_DIM)
            mixed = _dot(ws_ref[g], v[rows, cols]) + bs_ref[:, cols]
            uv_ref[rows, cols] = (u[rows, cols] * mixed).astype(BF16)
    o_ref[...] = x + _dot(uv_ref[...], wout_ref[...])


def _sg_layer(x, g, w_in, b_in, norm_g, w_s, b_full, w_out):
    rows = x.shape[0]
    tm = TM_SG
    row_spec = pl.BlockSpec((tm, D_MODEL), lambda i: (i, 0))
    return pl.pallas_call(
        functools.partial(_sg_kernel, tm=tm),
        grid=(rows // tm,),
        in_specs=[row_spec, _const_spec(g.shape), _const_spec(w_in.shape), _const_spec(b_in.shape),
                  _const_spec(norm_g.shape), _const_spec(w_s.shape), _const_spec(b_full.shape),
                  _const_spec(w_out.shape)],
        out_specs=row_spec,
        out_shape=jax.ShapeDtypeStruct(x.shape, F32),
        scratch_shapes=[pltpu.VMEM((tm, SG_HALF), BF16)],
        compiler_params=_params(),
        name="sg_mixer",
    )(x, g, w_in, b_in, norm_g, w_s, b_full, w_out)


def _ffn_kernel(xp_ref, x_ref, xn_ref, g_ref, wg_ref, wu_ref, cw_ref, cb_ref, wd_ref, fg_ref, o_ref,
                *, tm, tiles_per_seq, final):
    x = x_ref[...]
    xe = jnp.concatenate([xp_ref[...], x, xn_ref[...]], axis=0)
    he = _rms(xe, g_ref[...]).astype(BF16)
    gate = jnp.where(_edge_keep_mask(tm, tiles_per_seq), _dot(he, wg_ref[...]), 0.0)
    g_prev, g_mid, g_next = _neighbours(gate, tm)
    conv = g_prev * cw_ref[0:1, :] + g_mid * cw_ref[1:2, :] + g_next * cw_ref[2:3, :] + cb_ref[...]
    up = _dot(he[HALO:HALO + tm], wu_ref[...])
    hh = (_gelu(conv) * up).astype(BF16)
    y = x + _dot(hh, wd_ref[...])
    if final:
        y = _rms(y, fg_ref[...])
    o_ref[...] = y


def _ffn_layer(x, seq_len, g, w_gate, w_up, conv_w, conv_b, w_down, final_g, final):
    rows = x.shape[0]
    tm = TM_FFN
    row_spec = pl.BlockSpec((tm, D_MODEL), lambda i: (i, 0))
    prev_spec, next_spec = _halo_specs(tm, rows)
    return pl.pallas_call(
        functools.partial(_ffn_kernel, tm=tm, tiles_per_seq=seq_len // tm, final=final),
        grid=(rows // tm,),
        in_specs=[prev_spec, row_spec, next_spec, _const_spec(g.shape), _const_spec(w_gate.shape),
                  _const_spec(w_up.shape), _const_spec(conv_w.shape), _const_spec(conv_b.shape),
                  _const_spec(w_down.shape), _const_spec(final_g.shape)],
        out_specs=row_spec,
        out_shape=jax.ShapeDtypeStruct(x.shape, F32),
        compiler_params=_params(),
        name="conv_glu_ffn",
    )(x, x, x, g, w_gate, w_up, conv_w, conv_b, w_down, final_g)


def _softplus(z):
    return jnp.maximum(z, 0.0) + jnp.log1p(jnp.exp(-jnp.abs(z)))


def _tm_pre_kernel(xp_ref, x_ref, xn_ref, g_ref, mu_ref, wr_ref, wk_ref, wv_ref, g1_ref, g2_ref,
                   w1_ref, w2_ref, w0_ref, a1_ref, a2_ref, a0_ref, kk_ref, ka_ref, rk_ref, ones_ref,
                   r_o, v_o, kk_o, gate_o, bonus_o, lw0_o, kd0_o, b0_o, lw1_o, kd1_o, b1_o,
                   *, tm, tiles_per_seq):
    xe = jnp.concatenate([xp_ref[...], x_ref[...], xn_ref[...]], axis=0)
    he = jnp.where(_edge_keep_mask(tm, tiles_per_seq), _rms(xe, g_ref[...]), 0.0)
    h_prev, h, h_next = _neighbours(he, tm)
    xx = 0.5 * (h_prev + h_next) - h

    def mix(n):
        return (h + xx * mu_ref[n:n + 1, :]).astype(BF16)

    r = _dot(mix(0), wr_ref[...])
    k = _dot(mix(2), wk_ref[...])
    v = _dot(mix(3), wv_ref[...])
    gate_o[...] = _dot(jax.nn.sigmoid(_dot(mix(5), g1_ref[...])).astype(BF16), g2_ref[...])
    w_lora = jnp.tanh(_dot(mix(1), w1_ref[...])).astype(BF16)
    a_lora = _dot(mix(4), a1_ref[...]).astype(BF16)

    ones = ones_ref[...]
    kk_raw = k * kk_ref[...]
    kk = kk_raw / jnp.maximum(jnp.sqrt(_dot_hilo(kk_raw * kk_raw, ones)), L2_EPS)
    r_o[...] = r
    v_o[...] = v
    kk_o[...] = kk

    kd_sum = None
    for e, (lw_o, kd_o, b_o) in enumerate(((lw0_o, kd0_o, b0_o), (lw1_o, kd1_o, b1_o))):
        w_pre = _dot(w_lora, w2_ref[e]) + w0_ref[e:e + 1, :]
        lw_o[...] = -jnp.exp(-_softplus(-w_pre) - 0.5)
        a = jax.nn.sigmoid(_dot(a_lora, a2_ref[e]) + a0_ref[e:e + 1, :])
        kd = k * (1.0 + (a - 1.0) * ka_ref[...])
        kd_o[...] = kd
        b_o[...] = kk * a
        kd_sum = kd if kd_sum is None else kd_sum + kd
    bonus_o[...] = _dot_hilo(r * kd_sum * rk_ref[...], ones) * v


def _tm_pre(x, seq_len, g, p):
    rows = x.shape[0]
    tm = TM_TM
    row_spec = pl.BlockSpec((tm, D_MODEL), lambda i: (i, 0))
    prev_spec, next_spec = _halo_specs(tm, rows)
    consts = (g, p["mu"], p["w_r"], p["w_k"], p["w_v"], p["g1"], p["g2"], p["w1"], p["w2"], p["w0"],
              p["a1"], p["a2"], p["a0"], p["k_k"], p["k_a"], p["r_k"], p["ones"])
    return pl.pallas_call(
        functools.partial(_tm_pre_kernel, tm=tm, tiles_per_seq=seq_len // tm),
        grid=(rows // tm,),
        in_specs=[prev_spec, row_spec, next_spec] + [_const_spec(c.shape) for c in consts],
        out_specs=[row_spec] * 11,
        out_shape=[jax.ShapeDtypeStruct(x.shape, F32)] * 11,
        compiler_params=_params(),
        name="rwkv7_projections",
    )(x, x, x, *consts)


def _block_diag(x, left):
    zero = jnp.zeros_like(x)
    return jnp.concatenate([jnp.where(left, x, zero), jnp.where(left, zero, x)], axis=0)


def _scan_kernel(r_ref, v_ref, kk_ref, lw_ref, kd_ref, b_ref, y_ref,
                 state_ref, rq_ref, mp_ref, g_ref, decay_ref, *, rows, reverse):
    L = SCAN_L
    n_chunks = rows // L
    pairs = range(N_PAIRS)
    lanes = [slice(p * PAIR, (p + 1) * PAIR) for p in pairs]

    @pl.when(pl.program_id(1) == 0)
    def _():
        state_ref[...] = jnp.zeros_like(state_ref)

    t_idx = lax.broadcasted_iota(jnp.int32, (L, PAIR), 0)
    lane = lax.broadcasted_iota(jnp.int32, (L, PAIR), 1)
    s_idx = lane % L
    left = lane < HEAD
    if reverse:
        strict, incl = s_idx > t_idx, s_idx >= t_idx
    else:
        strict, incl = s_idx < t_idx, s_idx <= t_idx
    eye = jnp.where(s_idx == t_idx, 1.0, 0.0)
    left_state = lax.broadcasted_iota(jnp.int32, (HEAD, PAIR), 1) < HEAD
    sq_row = lax.broadcasted_iota(jnp.int32, (PAIR, PAIR), 0) < HEAD
    sq_col = lax.broadcasted_iota(jnp.int32, (PAIR, PAIR), 1) < HEAD
    same_head = sq_row == sq_col
    t_full = lax.broadcasted_iota(jnp.int32, (L, D_MODEL), 0)
    last = 0 if reverse else L - 1

    def bd(x):
        return _block_diag(x, left)

    def prepare(c, carry):
        rs = pl.ds(pl.multiple_of(c * L, L), L)
        lw = lw_ref[rs, :]
        cum = lw
        for s in (1, 2, 4, 8, 16, 32):
            if reverse:
                cum = cum + jnp.where(t_full < L - s, pltpu.roll(cum, L - s, axis=0), 0.0)
            else:
                cum = cum + jnp.where(t_full >= s, pltpu.roll(cum, s, axis=0), 0.0)
        half = 0.5 * cum[last:last + 1, :]
        e_half = jnp.exp(half)
        decay_ref[c] = e_half * e_half
        kk, r, v = kk_ref[rs, :], r_ref[rs, :], v_ref[rs, :]
        q_kappa = kk * jnp.exp(cum - lw - half)
        q_r = r * jnp.exp(cum - half)
        e_neg = jnp.exp(half - cum)
        b_t = b_ref[rs, :] * e_neg
        k_t = kd_ref[rs, :] * e_neg
        q_in = jnp.concatenate([q_kappa, q_r], axis=0).astype(BF16)
        bt16, kt16, v16 = b_t.astype(BF16), k_t.astype(BF16), v.astype(BF16)
        kf16 = (q_kappa * e_half).astype(BF16)
        r_full = q_r * e_half
        bend16 = (b_t * e_half).astype(BF16)
        kend16 = (k_t * e_half).astype(BF16)

        a_all = [_dot_nt(q_in[:, s], jnp.concatenate([bd(bt16[:, s]), bd(kt16[:, s])], axis=0)) for s in lanes]
        n_mat = [jnp.where(strict, a[:L, :PAIR], 0.0) for a in a_all]
        a_low = [jnp.concatenate([jnp.where(strict, a[:L, PAIR:], 0.0), jnp.where(incl, a[L:, PAIR:], 0.0)],
                                 axis=0).astype(BF16) for a in a_all]
        a_rb = [jnp.where(incl, a[L:, :PAIR], 0.0).astype(BF16) for a in a_all]
        av = [_dot(a_low[p], bd(v16[:, lanes[p]])) for p in pairs]
        t_inv = [eye - n for n in n_mat]
        qb = [(-n).astype(BF16) for n in n_mat]
        q = [_dot(qb[p], bd(qb[p])) for p in pairs]
        for _ in range(int(math.log2(L)) - 2):
            qb = [x.astype(BF16) for x in q]
            tq = [_dot(jnp.concatenate([t_inv[p].astype(BF16), qb[p]], axis=0), bd(qb[p])) for p in pairs]
            t_inv = [t_inv[p] + tq[p][:L] for p in pairs]
            q = [tq[p][L:] for p in pairs]
        t_inv = [t_inv[p] + _dot(t_inv[p].astype(BF16), bd(q[p].astype(BF16))) for p in pairs]
        tx = [_dot(t_inv[p].astype(BF16),
                   jnp.concatenate([bd(kf16[:, lanes[p]]), bd(av[p][:L].astype(BF16))], axis=1)) for p in pairs]
        kft16 = [x[:, :PAIR].astype(BF16) for x in tx]
        u016 = [(-x[:, PAIR:]).astype(BF16) for x in tx]
        ry = [_dot(a_rb[p], jnp.concatenate([bd(kft16[p]), bd(u016[p])], axis=1)) for p in pairs]
        zeros = jnp.zeros((L, PAIR), BF16)
        mg = [_dot_tn(jnp.concatenate([jnp.concatenate([kft16[p], u016[p]], axis=1),
                                       jnp.concatenate([zeros, v16[:, lanes[p]]], axis=1)], axis=0),
                      jnp.concatenate([bend16[:, lanes[p]], kend16[:, lanes[p]]], axis=0)) for p in pairs]
        for p in pairs:
            s = lanes[p]
            rq_ref[c, :, s] = (r_full[:, s] - ry[p][:, :PAIR]).astype(BF16)
            y_ref[rs, s] = ry[p][:, PAIR:] + av[p][L:]
            mp_ref[c, p] = jnp.where(same_head, -mg[p][:PAIR], 0.0).astype(BF16)
            g_ref[c, p] = jnp.where(left_state, mg[p][PAIR:PAIR + HEAD], mg[p][PAIR + HEAD:])
        return carry

    def advance(c, carry):
        cc = (n_chunks - 1 - c) if reverse else c
        rs = pl.ds(pl.multiple_of(cc * L, L), L)
        decay = decay_ref[cc]
        state = [state_ref[p] for p in pairs]
        sb = [x.astype(BF16) for x in state]
        carried = [_dot(sb[p], mp_ref[cc, p]) for p in pairs]
        from_state = [_dot_nt(rq_ref[cc, :, lanes[p]], _block_diag(sb[p], left_state)) for p in pairs]
        for p in pairs:
            state_ref[p] = state[p] * decay[:, lanes[p]] + carried[p] + g_ref[cc, p]
            y_ref[rs, lanes[p]] = y_ref[rs, lanes[p]] + from_state[p]
        return carry

    lax.fori_loop(0, n_chunks, prepare, 0)
    lax.fori_loop(0, n_chunks, advance, 0)


def _scan(r, v, kk, lw, kd, b, batch, seq_len, reverse):
    rows = SCAN_ROWS
    nblk = seq_len // rows

    def idx(bi, j):
        return (bi * nblk + ((nblk - 1 - j) if reverse else j), 0)

    spec = pl.BlockSpec((rows, D_MODEL), idx)
    return pl.pallas_call(
        functools.partial(_scan_kernel, rows=rows, reverse=reverse),
        grid=(batch, nblk),
        in_specs=[spec] * 6,
        out_specs=spec,
        out_shape=jax.ShapeDtypeStruct(r.shape, F32),
        scratch_shapes=[pltpu.VMEM((N_PAIRS, HEAD, PAIR), F32),
                        pltpu.VMEM((rows // SCAN_L, SCAN_L, D_MODEL), BF16),
                        pltpu.VMEM((rows // SCAN_L, N_PAIRS, PAIR, PAIR), BF16),
                        pltpu.VMEM((rows // SCAN_L, N_PAIRS, HEAD, PAIR), F32),
                        pltpu.VMEM((rows // SCAN_L, 1, D_MODEL), F32)],
        compiler_params=_params(2),
        name="rwkv7_scan_bwd" if reverse else "rwkv7_scan_fwd",
    )(r, v, kk, lw, kd, b)


def _tm_post_kernel(x_ref, yf_ref, yb_ref, bonus_ref, gate_ref, lng_ref, lnb_ref, ones_ref, wo_ref, o_ref):
    ones = ones_ref[...]
    y = yf_ref[...] + yb_ref[...]
    d = y - _dot_hilo(y, ones) * (1.0 / HEAD)
    var = _dot_hilo(d * d, ones) * (1.0 / HEAD)
    yn = d * lax.rsqrt(var + GN_EPS) * lng_ref[...] + lnb_ref[...]
    out = ((yn + bonus_ref[...]) * gate_ref[...]).astype(BF16)
    o_ref[...] = x_ref[...] + _dot(out, wo_ref[...])


def _tm_post(x, yf, yb, bonus, gate, p):
    rows = x.shape[0]
    tm = TM_TM
    row_spec = pl.BlockSpec((tm, D_MODEL), lambda i: (i, 0))
    consts = (p["ln_g"], p["ln_b"], p["ones"], p["w_o"])
    return pl.pallas_call(
        _tm_post_kernel,
        grid=(rows // tm,),
        in_specs=[row_spec] * 5 + [_const_spec(c.shape) for c in consts],
        out_specs=row_spec,
        out_shape=jax.ShapeDtypeStruct(x.shape, F32),
        compiler_params=_params(),
        name="rwkv7_output",
    )(x, yf, yb, bonus, gate, *consts)


def _row(v):
    return v.reshape(1, -1).astype(F32)


def _lora_out_padded(w2):
    z = jnp.zeros_like(w2[0])
    return jnp.stack([jnp.concatenate([w2[0], z], axis=0), jnp.concatenate([z, w2[1]], axis=0)]).astype(BF16)


def _prepare(norm_mix_g, norm_ffn_g, final_norm_g,
             sg_w_in, sg_b_in, sg_norm_g, sg_w_s, sg_b_s, sg_w_out,
             tm_mu, tm_w_r, tm_w_k, tm_w_v, tm_w0, tm_w1, tm_w2, tm_a0, tm_a1, tm_a2,
             tm_g1, tm_g2, tm_k_k, tm_k_a, tm_r_k, tm_ln_g, tm_ln_b, tm_w_o,
             ff_w_gate, ff_w_up, ff_conv_w, ff_conv_b, ff_w_down):
    head_id = jnp.arange(D_MODEL) // HEAD
    ones = (head_id[:, None] == head_id[None, :]).astype(BF16)
    sg, tm, ff = [], [], []
    for j in range(sg_w_in.shape[0]):
        sg.append(dict(
            w_in=sg_w_in[j].astype(BF16), b_in=_row(sg_b_in[j]), norm_g=_row(sg_norm_g[j]),
            w_s=sg_w_s[j].astype(BF16),
            b_full=jnp.repeat(sg_b_s[j].T, SG_GROUP_DIM, axis=1).astype(F32),
            w_out=sg_w_out[j].astype(BF16)))
    for j in range(tm_w_r.shape[0]):
        tm.append(dict(
            mu=tm_mu[j], w_r=tm_w_r[j].astype(BF16), w_k=tm_w_k[j].astype(BF16), w_v=tm_w_v[j].astype(BF16),
            g1=tm_g1[j].astype(BF16), g2=tm_g2[j].astype(BF16),
            w1=jnp.concatenate([tm_w1[j, 0], tm_w1[j, 1]], axis=1).astype(BF16), w2=_lora_out_padded(tm_w2[j]),
            w0=tm_w0[j],
            a1=jnp.concatenate([tm_a1[j, 0], tm_a1[j, 1]], axis=1).astype(BF16), a2=_lora_out_padded(tm_a2[j]),
            a0=tm_a0[j],
            k_k=_row(tm_k_k[j]), k_a=_row(tm_k_a[j]), r_k=_row(tm_r_k[j]),
            ln_g=_row(tm_ln_g[j]), ln_b=_row(tm_ln_b[j]), w_o=tm_w_o[j].astype(BF16), ones=ones))
    for i in range(ff_w_gate.shape[0]):
        ff.append(dict(w_gate=ff_w_gate[i].astype(BF16), w_up=ff_w_up[i].astype(BF16), conv_w=ff_conv_w[i],
                       conv_b=_row(ff_conv_b[i]), w_down=ff_w_down[i].astype(BF16)))
    return dict(mix_g=norm_mix_g, ffn_g=norm_ffn_g, final_g=_row(final_norm_g), sg=sg, tm=tm, ff=ff)


def _trunk(x3, prm):
    batch, seq_len, _ = x3.shape
    x = x3.reshape(batch * seq_len, D_MODEL)
    depth = prm["mix_g"].shape[0]
    for i in range(depth):
        g = _row(prm["mix_g"][i])
        if i % 2 == 0:
            p = prm["sg"][i // 2]
            x = _sg_layer(x, g, p["w_in"], p["b_in"], p["norm_g"], p["w_s"], p["b_full"], p["w_out"])
        else:
            p = prm["tm"][i // 2]
            r, v, kk, gate, bonus, lw0, kd0, b0, lw1, kd1, b1 = _tm_pre(x, seq_len, g, p)
            yf = _scan(r, v, kk, lw0, kd0, b0, batch, seq_len, False)
            yb = _scan(r, v, kk, lw1, kd1, b1, batch, seq_len, True)
            x = _tm_post(x, yf, yb, bonus, gate, p)
        f = prm["ff"][i]
        x = _ffn_layer(x, seq_len, _row(prm["ffn_g"][i]), f["w_gate"], f["w_up"], f["conv_w"], f["conv_b"],
                       f["w_down"], prm["final_g"], final=(i == depth - 1))
    return x.reshape(batch, seq_len, D_MODEL)


def kernel(x_prompt, x_sample, norm_mix_g, norm_ffn_g, final_norm_g, sg_w_in, sg_b_in, sg_norm_g, sg_w_s, sg_b_s, sg_w_out, tm_mu, tm_w_r, tm_w_k, tm_w_v, tm_w0, tm_w1, tm_w2, tm_a0, tm_a1, tm_a2, tm_g1, tm_g2, tm_k_k, tm_k_a, tm_r_k, tm_ln_g, tm_ln_b, tm_w_o, ff_w_gate, ff_w_up, ff_conv_w, ff_conv_b, ff_w_down):
    prm = _prepare(norm_mix_g, norm_ffn_g, final_norm_g, sg_w_in, sg_b_in, sg_norm_g, sg_w_s, sg_b_s, sg_w_out,
                   tm_mu, tm_w_r, tm_w_k, tm_w_v, tm_w0, tm_w1, tm_w2, tm_a0, tm_a1, tm_a2,
                   tm_g1, tm_g2, tm_k_k, tm_k_a, tm_r_k, tm_ln_g, tm_ln_b, tm_w_o,
                   ff_w_gate, ff_w_up, ff_conv_w, ff_conv_b, ff_w_down)
    return (_trunk(x_prompt, prm), _trunk(x_sample, prm))
```

```python
import functools
import math

import jax
import jax.numpy as jnp
from jax import lax
from jax.experimental import pallas as pl
from jax.experimental.pallas import tpu as pltpu

F32 = jnp.float32
BF16 = jnp.bfloat16

D_MODEL = 1024
SG_CHUNK = 128
SG_HALF = 2 * D_MODEL
SG_GROUPS = 8
SG_GROUP_DIM = SG_HALF // SG_GROUPS
HEAD = 64
PAIR = 2 * HEAD
N_PAIRS = D_MODEL // PAIR
LORA_W = 64
LORA_A = 64
GN_EPS = 64e-5
RMS_EPS = 1e-6
L2_EPS = 1e-12
HALO = 8
SCAN_L = 64
VMEM_LIMIT = 56 * 1024 * 1024

TM_SG = 256
TM_FFN = 256
TM_TM = 256
SCAN_ROWS = 256
PREP_CHUNKS = 2


def _rms(x, g):
    return x * lax.rsqrt(jnp.mean(x * x, axis=-1, keepdims=True) + RMS_EPS) * g


def _gelu(x):
    return 0.5 * x * (1.0 + lax.erf(x * (1.0 / math.sqrt(2.0))))


def _dot(a, b):
    return jnp.dot(a, b, preferred_element_type=F32)


def _dot_nt(a, b):
    return lax.dot_general(a, b, (((1,), (1,)), ((), ())), preferred_element_type=F32)


def _dot_tn(a, b):
    return lax.dot_general(a, b, (((0,), (0,)), ((), ())), preferred_element_type=F32)


def _head_sum(a, ones_bf16):
    return _dot(a.astype(BF16), ones_bf16)


def _const_spec(shape):
    nd = len(shape)
    return pl.BlockSpec(shape, lambda *_: (0,) * nd, pipeline_mode=pl.Buffered(1))


def _params(n_axes=1):
    return pltpu.CompilerParams(dimension_semantics=("arbitrary",) * n_axes,
                                vmem_limit_bytes=VMEM_LIMIT)


def _halo_specs(tm, total_rows):
    per = tm // HALO
    last = total_rows // HALO - 1
    prev = pl.BlockSpec((HALO, D_MODEL), lambda i: (jnp.maximum(i * per - 1, 0), 0))
    nxt = pl.BlockSpec((HALO, D_MODEL), lambda i: (jnp.minimum((i + 1) * per, last), 0))
    return prev, nxt


def _edge_keep_mask(tm, tiles_per_seq):
    i = pl.program_id(0)
    pos = i % tiles_per_seq
    row = lax.broadcasted_iota(jnp.int32, (tm + 2 * HALO, 1), 0)
    drop = jnp.logical_or(jnp.logical_and(pos == 0, row < HALO),
                          jnp.logical_and(pos == tiles_per_seq - 1, row >= tm + HALO))
    return jnp.logical_not(drop)


def _neighbours(ext, tm):
    n = tm + 2 * HALO
    prev = pltpu.roll(ext, 1, axis=0)[HALO:HALO + tm]
    nxt = pltpu.roll(ext, n - 1, axis=0)[HALO:HALO + tm]
    return prev, ext[HALO:HALO + tm], nxt


def _sg_kernel(x_ref, g_ref, win_ref, bin_ref, ng_ref, ws_ref, bs_ref, wout_ref, o_ref, uv_ref, *, tm):
    x = x_ref[...]
    h = _rms(x, g_ref[...]).astype(BF16)
    z = _gelu(_dot(h, win_ref[...]) + bin_ref[...])
    u = z[:, :SG_HALF]
    v = _rms(z[:, SG_HALF:], ng_ref[...]).astype(BF16)
    for c in range(tm // SG_CHUNK):
        rows = slice(c * SG_CHUNK, (c + 1) * SG_CHUNK)
        for g in range(SG_GROUPS):
            cols = slice(g * SG_GROUP_DIM, (g + 1) * SG_GROUP_DIM)
            mixed = _dot(ws_ref[g], v[rows, cols]) + bs_ref[:, cols]
            uv_ref[rows, cols] = (u[rows, cols] * mixed).astype(BF16)
    o_ref[...] = x + _dot(uv_ref[...], wout_ref[...])


def _sg_layer(x, g, w_in, b_in, norm_g, w_s, b_full, w_out):
    rows = x.shape[0]
    tm = TM_SG
    row_spec = pl.BlockSpec((tm, D_MODEL), lambda i: (i, 0))
    return pl.pallas_call(
        functools.partial(_sg_kernel, tm=tm),
        grid=(rows // tm,),
        in_specs=[row_spec, _const_spec(g.shape), _const_spec(w_in.shape), _const_spec(b_in.shape),
                  _const_spec(norm_g.shape), _const_spec(w_s.shape), _const_spec(b_full.shape),
                  _const_spec(w_out.shape)],
        out_specs=row_spec,
        out_shape=jax.ShapeDtypeStruct(x.shape, F32),
        scratch_shapes=[pltpu.VMEM((tm, SG_HALF), BF16)],
        compiler_params=_params(),
        name="sg_mixer",
    )(x, g, w_in, b_in, norm_g, w_s, b_full, w_out)


def _ffn_kernel(xp_ref, x_ref, xn_ref, g_ref, wg_ref, wu_ref, cw_ref, cb_ref, wd_ref, fg_ref, o_ref,
                *, tm, tiles_per_seq, final):
    x = x_ref[...]
    xe = jnp.concatenate([xp_ref[...], x, xn_ref[...]], axis=0)
    he = _rms(xe, g_ref[...]).astype(BF16)
    gate = jnp.where(_edge_keep_mask(tm, tiles_per_seq), _dot(he, wg_ref[...]), 0.0)
    g_prev, g_mid, g_next = _neighbours(gate, tm)
    conv = g_prev * cw_ref[0:1, :] + g_mid * cw_ref[1:2, :] + g_next * cw_ref[2:3, :] + cb_ref[...]
    up = _dot(he[HALO:HALO + tm], wu_ref[...])
    hh = (_gelu(conv) * up).astype(BF16)
    y = x + _dot(hh, wd_ref[...])
    if final:
        y = _rms(y, fg_ref[...])
    o_ref[...] = y


def _ffn_layer(x, seq_len, g, w_gate, w_up, conv_w, conv_b, w_down, final_g, final):
    rows = x.shape[0]
    tm = TM_FFN
    row_spec = pl.BlockSpec((tm, D_MODEL), lambda i: (i, 0))
    prev_spec, next_spec = _halo_specs(tm, rows)
    return pl.pallas_call(
        functools.partial(_ffn_kernel, tm=tm, tiles_per_seq=seq_len // tm, final=final),
        grid=(rows // tm,),
        in_specs=[prev_spec, row_spec, next_spec, _const_spec(g.shape), _const_spec(w_gate.shape),
                  _const_spec(w_up.shape), _const_spec(conv_w.shape), _const_spec(conv_b.shape),
                  _const_spec(w_down.shape), _const_spec(final_g.shape)],
        out_specs=row_spec,
        out_shape=jax.ShapeDtypeStruct(x.shape, F32),
        compiler_params=_params(),
        name="conv_glu_ffn",
    )(x, x, x, g, w_gate, w_up, conv_w, conv_b, w_down, final_g)


def _softplus(z):
    return jnp.maximum(z, 0.0) + jnp.log1p(jnp.exp(-jnp.abs(z)))


def _tm_pre_kernel(xp_ref, x_ref, xn_ref, g_ref, mu_ref, wr_ref, wk_ref, wv_ref, g1_ref, g2_ref,
                   w1_ref, w2_ref, w0_ref, a1_ref, a2_ref, a0_ref, kk_ref, ka_ref, rk_ref, ones_ref,
                   r_o, v_o, kk_o, gate_o, bonus_o, lw0_o, kd0_o, b0_o, lw1_o, kd1_o, b1_o,
                   *, tm, tiles_per_seq):
    xe = jnp.concatenate([xp_ref[...], x_ref[...], xn_ref[...]], axis=0)
    he = jnp.where(_edge_keep_mask(tm, tiles_per_seq), _rms(xe, g_ref[...]), 0.0)
    h_prev, h, h_next = _neighbours(he, tm)
    xx = 0.5 * (h_prev + h_next) - h

    def mix(n):
        return (h + xx * mu_ref[n:n + 1, :]).astype(BF16)

    r = _dot(mix(0), wr_ref[...])
    k = _dot(mix(2), wk_ref[...])
    v = _dot(mix(3), wv_ref[...])
    gate_o[...] = _dot(jax.nn.sigmoid(_dot(mix(5), g1_ref[...])).astype(BF16), g2_ref[...])
    w_lora = jnp.tanh(_dot(mix(1), w1_ref[...])).astype(BF16)
    a_lora = _dot(mix(4), a1_ref[...]).astype(BF16)

    ones = ones_ref[...]
    kk_raw = k * kk_ref[...]
    kk = kk_raw / jnp.maximum(jnp.sqrt(_head_sum(kk_raw * kk_raw, ones)), L2_EPS)
    r_o[...] = r
    v_o[...] = v
    kk_o[...] = kk

    kd_sum = None
    for e, (lw_o, kd_o, b_o) in enumerate(((lw0_o, kd0_o, b0_o), (lw1_o, kd1_o, b1_o))):
        w_pre = _dot(w_lora, w2_ref[e]) + w0_ref[e:e + 1, :]
        lw_o[...] = -jnp.exp(-_softplus(-w_pre) - 0.5)
        a = jax.nn.sigmoid(_dot(a_lora, a2_ref[e]) + a0_ref[e:e + 1, :])
        kd = k * (1.0 + (a - 1.0) * ka_ref[...])
        kd_o[...] = kd
        b_o[...] = kk * a
        kd_sum = kd if kd_sum is None else kd_sum + kd
    bonus_o[...] = _head_sum(r * kd_sum * rk_ref[...], ones) * v


def _tm_pre(x, seq_len, g, p):
    rows = x.shape[0]
    tm = TM_TM
    row_spec = pl.BlockSpec((tm, D_MODEL), lambda i: (i, 0))
    prev_spec, next_spec = _halo_specs(tm, rows)
    consts = (g, p["mu"], p["w_r"], p["w_k"], p["w_v"], p["g1"], p["g2"], p["w1"], p["w2"], p["w0"],
              p["a1"], p["a2"], p["a0"], p["k_k"], p["k_a"], p["r_k"], p["ones"])
    return pl.pallas_call(
        functools.partial(_tm_pre_kernel, tm=tm, tiles_per_seq=seq_len // tm),
        grid=(rows // tm,),
        in_specs=[prev_spec, row_spec, next_spec] + [_const_spec(c.shape) for c in consts],
        out_specs=[row_spec] * 11,
        out_shape=[jax.ShapeDtypeStruct(x.shape, F32)] * 11,
        compiler_params=_params(),
        name="rwkv7_projections",
    )(x, x, x, *consts)


def _block_diag(x, left):
    zero = jnp.zeros_like(x)
    return jnp.concatenate([jnp.where(left, x, zero), jnp.where(left, zero, x)], axis=0)


def _scan_kernel(r_ref, v_ref, kk_ref, lw_ref, kd_ref, b_ref, y_ref,
                 state_ref, rq_ref, mp_ref, g_ref, decay_ref, *, rows, reverse):
    L = SCAN_L
    n_chunks = rows // L
    pairs = range(N_PAIRS)
    lanes = [slice(p * PAIR, (p + 1) * PAIR) for p in pairs]

    @pl.when(pl.program_id(1) == 0)
    def _():
        state_ref[...] = jnp.zeros_like(state_ref)

    t_idx = lax.broadcasted_iota(jnp.int32, (L, PAIR), 0)
    lane = lax.broadcasted_iota(jnp.int32, (L, PAIR), 1)
    s_idx = lane % L
    left = lane < HEAD
    if reverse:
        strict, incl = s_idx > t_idx, s_idx >= t_idx
    else:
        strict, incl = s_idx < t_idx, s_idx <= t_idx
    eye = jnp.where(s_idx == t_idx, 1.0, 0.0)
    left_state = lax.broadcasted_iota(jnp.int32, (HEAD, PAIR), 1) < HEAD
    sq_row = lax.broadcasted_iota(jnp.int32, (PAIR, PAIR), 0) < HEAD
    sq_col = lax.broadcasted_iota(jnp.int32, (PAIR, PAIR), 1) < HEAD
    same_head = sq_row == sq_col
    t_full = lax.broadcasted_iota(jnp.int32, (L, D_MODEL), 0)
    last = 0 if reverse else L - 1

    def bd(x):
        return _block_diag(x, left)

    def chunk_operands(c):
        rs = pl.ds(pl.multiple_of(c * L, L), L)
        lw = lw_ref[rs, :]
        cum = lw
        for s in (1, 2, 4, 8, 16, 32):
            if reverse:
                cum = cum + jnp.where(t_full < L - s, pltpu.roll(cum, L - s, axis=0), 0.0)
            else:
                cum = cum + jnp.where(t_full >= s, pltpu.roll(cum, s, axis=0), 0.0)
        half = 0.5 * cum[last:last + 1, :]
        e_half = jnp.exp(half)
        decay_ref[c] = e_half * e_half
        q_kappa = kk_ref[rs, :] * jnp.exp(cum - lw - half)
        q_r = r_ref[rs, :] * jnp.exp(cum - half)
        e_neg = jnp.exp(half - cum)
        b_t = b_ref[rs, :] * e_neg
        k_t = kd_ref[rs, :] * e_neg
        return dict(
            rs=rs, c=c,
            q_in=jnp.concatenate([q_kappa, q_r], axis=0).astype(BF16),
            bt=b_t.astype(BF16), kt=k_t.astype(BF16), v=v_ref[rs, :].astype(BF16),
            kf=(q_kappa * e_half).astype(BF16),
            r_full=q_r * e_half,
            bend=(b_t * e_half).astype(BF16),
            kend=(k_t * e_half).astype(BF16))

    def prepare(c2, carry):
        ops = [chunk_operands(c2 * PREP_CHUNKS + i) for i in range(PREP_CHUNKS)]
        probs = [(o, s) for o in ops for s in lanes]
        n = range(len(probs))
        a_all = [_dot_nt(o["q_in"][:, s], jnp.concatenate([bd(o["bt"][:, s]), bd(o["kt"][:, s])], axis=0))
                 for o, s in probs]
        n_mat = [jnp.where(strict, a[:L, :PAIR], 0.0) for a in a_all]
        a_low = [jnp.concatenate([jnp.where(strict, a[:L, PAIR:], 0.0), jnp.where(incl, a[L:, PAIR:], 0.0)],
                                 axis=0).astype(BF16) for a in a_all]
        a_rb = [jnp.where(incl, a[L:, :PAIR], 0.0).astype(BF16) for a in a_all]
        av = [_dot(a_low[i], bd(probs[i][0]["v"][:, probs[i][1]])) for i in n]
        t_inv = [eye - x for x in n_mat]
        qb = [(-x).astype(BF16) for x in n_mat]
        q = [_dot(qb[i], bd(qb[i])) for i in n]
        for _ in range(int(math.log2(L)) - 2):
            qb = [x.astype(BF16) for x in q]
            tq = [_dot(jnp.concatenate([t_inv[i].astype(BF16), qb[i]], axis=0), bd(qb[i])) for i in n]
            t_inv = [t_inv[i] + tq[i][:L] for i in n]
            q = [tq[i][L:] for i in n]
        t_inv = [t_inv[i] + _dot(t_inv[i].astype(BF16), bd(q[i].astype(BF16))) for i in n]
        tx = [_dot(t_inv[i].astype(BF16),
                   jnp.concatenate([bd(probs[i][0]["kf"][:, probs[i][1]]), bd(av[i][:L].astype(BF16))], axis=1))
              for i in n]
        kft16 = [x[:, :PAIR].astype(BF16) for x in tx]
        u016 = [(-x[:, PAIR:]).astype(BF16) for x in tx]
        ry = [_dot(a_rb[i], jnp.concatenate([bd(kft16[i]), bd(u016[i])], axis=1)) for i in n]
        zeros = jnp.zeros((L, PAIR), BF16)
        mg = [_dot_tn(jnp.concatenate([jnp.concatenate([kft16[i], u016[i]], axis=1),
                                       jnp.concatenate([zeros, probs[i][0]["v"][:, probs[i][1]]], axis=1)], axis=0),
                      jnp.concatenate([probs[i][0]["bend"][:, probs[i][1]], probs[i][0]["kend"][:, probs[i][1]]],
                                      axis=0)) for i in n]
        for i in n:
            o, s = probs[i]
            p = i % N_PAIRS
            rq_ref[o["c"], :, s] = (o["r_full"][:, s] - ry[i][:, :PAIR]).astype(BF16)
            y_ref[o["rs"], s] = ry[i][:, PAIR:] + av[i][L:]
            mp_ref[o["c"], p] = jnp.where(same_head, -mg[i][:PAIR], 0.0).astype(BF16)
            g_ref[o["c"], p] = jnp.where(left_state, mg[i][PAIR:PAIR + HEAD], mg[i][PAIR + HEAD:])
        return carry

    lax.fori_loop(0, n_chunks // PREP_CHUNKS, prepare, 0)

    state = [state_ref[p] for p in pairs]
    for c in range(n_chunks):
        cc = (n_chunks - 1 - c) if reverse else c
        rs = slice(cc * L, (cc + 1) * L)
        decay = decay_ref[cc]
        sb = [x.astype(BF16) for x in state]
        carried = [_dot(sb[p], mp_ref[cc, p]) for p in pairs]
        from_state = [_dot_nt(rq_ref[cc, :, lanes[p]], _block_diag(sb[p], left_state)) for p in pairs]
        state = [state[p] * decay[:, lanes[p]] + carried[p] + g_ref[cc, p] for p in pairs]
        for p in pairs:
            y_ref[rs, lanes[p]] = y_ref[rs, lanes[p]] + from_state[p]
    for p in pairs:
        state_ref[p] = state[p]


def _scan(r, v, kk, lw, kd, b, batch, seq_len, reverse):
    rows = SCAN_ROWS
    nblk = seq_len // rows

    def idx(bi, j):
        return (bi * nblk + ((nblk - 1 - j) if reverse else j), 0)

    spec = pl.BlockSpec((rows, D_MODEL), idx)
    return pl.pallas_call(
        functools.partial(_scan_kernel, rows=rows, reverse=reverse),
        grid=(batch, nblk),
        in_specs=[spec] * 6,
        out_specs=spec,
        out_shape=jax.ShapeDtypeStruct(r.shape, F32),
        scratch_shapes=[pltpu.VMEM((N_PAIRS, HEAD, PAIR), F32),
                        pltpu.VMEM((rows // SCAN_L, SCAN_L, D_MODEL), BF16),
                        pltpu.VMEM((rows // SCAN_L, N_PAIRS, PAIR, PAIR), BF16),
                        pltpu.VMEM((rows // SCAN_L, N_PAIRS, HEAD, PAIR), F32),
                        pltpu.VMEM((rows // SCAN_L, 1, D_MODEL), F32)],
        compiler_params=_params(2),
        name="rwkv7_scan_bwd" if reverse else "rwkv7_scan_fwd",
    )(r, v, kk, lw, kd, b)


def _tm_post_kernel(x_ref, yf_ref, yb_ref, bonus_ref, gate_ref, lng_ref, lnb_ref, ones_ref, wo_ref, o_ref):
    ones = ones_ref[...]
    y = yf_ref[...] + yb_ref[...]
    d = y - _head_sum(y, ones) * (1.0 / HEAD)
    var = _head_sum(d * d, ones) * (1.0 / HEAD)
    yn = d * lax.rsqrt(var + GN_EPS) * lng_ref[...] + lnb_ref[...]
    out = ((yn + bonus_ref[...]) * gate_ref[...]).astype(BF16)
    o_ref[...] = x_ref[...] + _dot(out, wo_ref[...])


def _tm_post(x, yf, yb, bonus, gate, p):
    rows = x.shape[0]
    tm = TM_TM
    row_spec = pl.BlockSpec((tm, D_MODEL), lambda i: (i, 0))
    consts = (p["ln_g"], p["ln_b"], p["ones"], p["w_o"])
    return pl.pallas_call(
        _tm_post_kernel,
        grid=(rows // tm,),
        in_specs=[row_spec] * 5 + [_const_spec(c.shape) for c in consts],
        out_specs=row_spec,
        out_shape=jax.ShapeDtypeStruct(x.shape, F32),
        compiler_params=_params(),
        name="rwkv7_output",
    )(x, yf, yb, bonus, gate, *consts)


def _row(v):
    return v.reshape(1, -1).astype(F32)


def _lora_out_padded(w2):
    z = jnp.zeros_like(w2[0])
    return jnp.stack([jnp.concatenate([w2[0], z], axis=0), jnp.concatenate([z, w2[1]], axis=0)]).astype(BF16)


def _prepare(norm_mix_g, norm_ffn_g, final_norm_g,
             sg_w_in, sg_b_in, sg_norm_g, sg_w_s, sg_b_s, sg_w_out,
             tm_mu, tm_w_r, tm_w_k, tm_w_v, tm_w0, tm_w1, tm_w2, tm_a0, tm_a1, tm_a2,
             tm_g1, tm_g2, tm_k_k, tm_k_a, tm_r_k, tm_ln_g, tm_ln_b, tm_w_o,
             ff_w_gate, ff_w_up, ff_conv_w, ff_conv_b, ff_w_down):
    head_id = jnp.arange(D_MODEL) // HEAD
    ones = (head_id[:, None] == head_id[None, :]).astype(BF16)
    sg, tm, ff = [], [], []
    for j in range(sg_w_in.shape[0]):
        sg.append(dict(
            w_in=sg_w_in[j].astype(BF16), b_in=_row(sg_b_in[j]), norm_g=_row(sg_norm_g[j]),
            w_s=sg_w_s[j].astype(BF16),
            b_full=jnp.repeat(sg_b_s[j].T, SG_GROUP_DIM, axis=1).astype(F32),
            w_out=sg_w_out[j].astype(BF16)))
    for j in range(tm_w_r.shape[0]):
        tm.append(dict(
            mu=tm_mu[j], w_r=tm_w_r[j].astype(BF16), w_k=tm_w_k[j].astype(BF16), w_v=tm_w_v[j].astype(BF16),
            g1=tm_g1[j].astype(BF16), g2=tm_g2[j].astype(BF16),
            w1=jnp.concatenate([tm_w1[j, 0], tm_w1[j, 1]], axis=1).astype(BF16), w2=_lora_out_padded(tm_w2[j]),
            w0=tm_w0[j],
            a1=jnp.concatenate([tm_a1[j, 0], tm_a1[j, 1]], axis=1).astype(BF16), a2=_lora_out_padded(tm_a2[j]),
            a0=tm_a0[j],
            k_k=_row(tm_k_k[j]), k_a=_row(tm_k_a[j]), r_k=_row(tm_r_k[j]),
            ln_g=_row(tm_ln_g[j]), ln_b=_row(tm_ln_b[j]), w_o=tm_w_o[j].astype(BF16), ones=ones))
    for i in range(ff_w_gate.shape[0]):
        ff.append(dict(w_gate=ff_w_gate[i].astype(BF16), w_up=ff_w_up[i].astype(BF16), conv_w=ff_conv_w[i],
                       conv_b=_row(ff_conv_b[i]), w_down=ff_w_down[i].astype(BF16)))
    return dict(mix_g=norm_mix_g, ffn_g=norm_ffn_g, final_g=_row(final_norm_g), sg=sg, tm=tm, ff=ff)


def _trunk(x3, prm):
    batch, seq_len, _ = x3.shape
    x = x3.reshape(batch * seq_len, D_MODEL)
    depth = prm["mix_g"].shape[0]
    for i in range(depth):
        g = _row(prm["mix_g"][i])
        if i % 2 == 0:
            p = prm["sg"][i // 2]
            x = _sg_layer(x, g, p["w_in"], p["b_in"], p["norm_g"], p["w_s"], p["b_full"], p["w_out"])
        else:
            p = prm["tm"][i // 2]
            r, v, kk, gate, bonus, lw0, kd0, b0, lw1, kd1, b1 = _tm_pre(x, seq_len, g, p)
            yf = _scan(r, v, kk, lw0, kd0, b0, batch, seq_len, False)
            yb = _scan(r, v, kk, lw1, kd1, b1, batch, seq_len, True)
            x = _tm_post(x, yf, yb, bonus, gate, p)
        f = prm["ff"][i]
        x = _ffn_layer(x, seq_len, _row(prm["ffn_g"][i]), f["w_gate"], f["w_up"], f["conv_w"], f["conv_b"],
                       f["w_down"], prm["final_g"], final=(i == depth - 1))
    return x.reshape(batch, seq_len, D_MODEL)


def kernel(x_prompt, x_sample, norm_mix_g, norm_ffn_g, final_norm_g, sg_w_in, sg_b_in, sg_norm_g, sg_w_s, sg_b_s, sg_w_out, tm_mu, tm_w_r, tm_w_k, tm_w_v, tm_w0, tm_w1, tm_w2, tm_a0, tm_a1, tm_a2, tm_g1, tm_g2, tm_k_k, tm_k_a, tm_r_k, tm_ln_g, tm_ln_b, tm_w_o, ff_w_gate, ff_w_up, ff_conv_w, ff_conv_b, ff_w_down):
    prm = _prepare(norm_mix_g, norm_ffn_g, final_norm_g, sg_w_in, sg_b_in, sg_norm_g, sg_w_s, sg_b_s, sg_w_out,
                   tm_mu, tm_w_r, tm_w_k, tm_w_v, tm_w0, tm_w1, tm_w2, tm_a0, tm_a1, tm_a2,
                   tm_g1, tm_g2, tm_k_k, tm_k_a, tm_r_k, tm_ln_g, tm_ln_b, tm_w_o,
                   ff_w_gate, ff_w_up, ff_conv_w, ff_conv_b, ff_w_down)
    return (_trunk(x_prompt, prm), _trunk(x_sample, prm))
```

```python
import functools
import math

import jax
import jax.numpy as jnp
from jax import lax
from jax.experimental import pallas as pl
from jax.experimental.pallas import tpu as pltpu

F32 = jnp.float32
BF16 = jnp.bfloat16

D_MODEL = 1024
SG_CHUNK = 128
SG_HALF = 2 * D_MODEL
SG_GROUPS = 8
SG_GROUP_DIM = SG_HALF // SG_GROUPS
HEAD = 64
PAIR = 2 * HEAD
N_PAIRS = D_MODEL // PAIR
LORA_W = 64
LORA_A = 64
GN_EPS = 64e-5
RMS_EPS = 1e-6
L2_EPS = 1e-12
HALO = 8
SCAN_L = 64
VMEM_LIMIT = 56 * 1024 * 1024

TM_SG = 256
TM_FFN = 512
TM_TM = 256
SCAN_ROWS = 256
PREP_CHUNKS = 4


def _rms(x, g):
    return x * lax.rsqrt(jnp.mean(x * x, axis=-1, keepdims=True) + RMS_EPS) * g


def _gelu(x):
    return 0.5 * x * (1.0 + lax.erf(x * (1.0 / math.sqrt(2.0))))


def _dot(a, b):
    return jnp.dot(a, b, preferred_element_type=F32)


def _dot_nt(a, b):
    return lax.dot_general(a, b, (((1,), (1,)), ((), ())), preferred_element_type=F32)


def _dot_tn(a, b):
    return lax.dot_general(a, b, (((0,), (0,)), ((), ())), preferred_element_type=F32)


def _head_sum(a, ones_bf16):
    return _dot(a.astype(BF16), ones_bf16)


def _const_spec(shape):
    nd = len(shape)
    return pl.BlockSpec(shape, lambda *_: (0,) * nd, pipeline_mode=pl.Buffered(1))


def _params(n_axes=1):
    return pltpu.CompilerParams(dimension_semantics=("arbitrary",) * n_axes,
                                vmem_limit_bytes=VMEM_LIMIT)


def _halo_specs(tm, total_rows):
    per = tm // HALO
    last = total_rows // HALO - 1
    prev = pl.BlockSpec((HALO, D_MODEL), lambda i: (jnp.maximum(i * per - 1, 0), 0))
    nxt = pl.BlockSpec((HALO, D_MODEL), lambda i: (jnp.minimum((i + 1) * per, last), 0))
    return prev, nxt


def _edge_keep_mask(tm, tiles_per_seq):
    i = pl.program_id(0)
    pos = i % tiles_per_seq
    row = lax.broadcasted_iota(jnp.int32, (tm + 2 * HALO, 1), 0)
    drop = jnp.logical_or(jnp.logical_and(pos == 0, row < HALO),
                          jnp.logical_and(pos == tiles_per_seq - 1, row >= tm + HALO))
    return jnp.logical_not(drop)


def _neighbours(ext, tm):
    n = tm + 2 * HALO
    prev = pltpu.roll(ext, 1, axis=0)[HALO:HALO + tm]
    nxt = pltpu.roll(ext, n - 1, axis=0)[HALO:HALO + tm]
    return prev, ext[HALO:HALO + tm], nxt


def _sg_kernel(x_ref, g_ref, win_ref, bin_ref, ng_ref, ws_ref, bs_ref, wout_ref, o_ref, uv_ref, *, tm):
    x = x_ref[...]
    h = _rms(x, g_ref[...]).astype(BF16)
    z = _gelu(_dot(h, win_ref[...]) + bin_ref[...])
    u = z[:, :SG_HALF]
    v = _rms(z[:, SG_HALF:], ng_ref[...]).astype(BF16)
    for c in range(tm // SG_CHUNK):
        rows = slice(c * SG_CHUNK, (c + 1) * SG_CHUNK)
        for g in range(SG_GROUPS):
            cols = slice(g * SG_GROUP_DIM, (g + 1) * SG_GROUP_DIM)
            mixed = _dot(ws_ref[g], v[rows, cols]) + bs_ref[:, cols]
            uv_ref[rows, cols] = (u[rows, cols] * mixed).astype(BF16)
    o_ref[...] = x + _dot(uv_ref[...], wout_ref[...])


def _sg_layer(x, g, w_in, b_in, norm_g, w_s, b_full, w_out):
    rows = x.shape[0]
    tm = TM_SG
    row_spec = pl.BlockSpec((tm, D_MODEL), lambda i: (i, 0))
    return pl.pallas_call(
        functools.partial(_sg_kernel, tm=tm),
        grid=(rows // tm,),
        in_specs=[row_spec, _const_spec(g.shape), _const_spec(w_in.shape), _const_spec(b_in.shape),
                  _const_spec(norm_g.shape), _const_spec(w_s.shape), _const_spec(b_full.shape),
                  _const_spec(w_out.shape)],
        out_specs=row_spec,
        out_shape=jax.ShapeDtypeStruct(x.shape, F32),
        scratch_shapes=[pltpu.VMEM((tm, SG_HALF), BF16)],
        compiler_params=_params(),
        name="sg_mixer",
    )(x, g, w_in, b_in, norm_g, w_s, b_full, w_out)


def _ffn_kernel(xp_ref, x_ref, xn_ref, g_ref, wg_ref, wu_ref, cw_ref, cb_ref, wd_ref, fg_ref, o_ref,
                *, tm, tiles_per_seq, final):
    x = x_ref[...]
    xe = jnp.concatenate([xp_ref[...], x, xn_ref[...]], axis=0)
    he = _rms(xe, g_ref[...]).astype(BF16)
    gate = jnp.where(_edge_keep_mask(tm, tiles_per_seq), _dot(he, wg_ref[...]), 0.0)
    g_prev, g_mid, g_next = _neighbours(gate, tm)
    conv = g_prev * cw_ref[0:1, :] + g_mid * cw_ref[1:2, :] + g_next * cw_ref[2:3, :] + cb_ref[...]
    up = _dot(he[HALO:HALO + tm], wu_ref[...])
    hh = (_gelu(conv) * up).astype(BF16)
    y = x + _dot(hh, wd_ref[...])
    if final:
        y = _rms(y, fg_ref[...])
    o_ref[...] = y


def _ffn_layer(x, seq_len, g, w_gate, w_up, conv_w, conv_b, w_down, final_g, final):
    rows = x.shape[0]
    tm = TM_FFN
    row_spec = pl.BlockSpec((tm, D_MODEL), lambda i: (i, 0))
    prev_spec, next_spec = _halo_specs(tm, rows)
    return pl.pallas_call(
        functools.partial(_ffn_kernel, tm=tm, tiles_per_seq=seq_len // tm, final=final),
        grid=(rows // tm,),
        in_specs=[prev_spec, row_spec, next_spec, _const_spec(g.shape), _const_spec(w_gate.shape),
                  _const_spec(w_up.shape), _const_spec(conv_w.shape), _const_spec(conv_b.shape),
                  _const_spec(w_down.shape), _const_spec(final_g.shape)],
        out_specs=row_spec,
        out_shape=jax.ShapeDtypeStruct(x.shape, F32),
        compiler_params=_params(),
        name="conv_glu_ffn",
    )(x, x, x, g, w_gate, w_up, conv_w, conv_b, w_down, final_g)


def _tm_pre_kernel(xp_ref, x_ref, xn_ref, g_ref, mu_ref, wr_ref, wk_ref, wv_ref, g1_ref, g2_ref,
                   w1_ref, w2_ref, w0_ref, a1_ref, a2_ref, a0_ref, kk_ref, ka_ref, rk_ref, ones_ref,
                   r_o, v_o, kk_o, gate_o, bonus_o, lw0_o, kd0_o, b0_o, lw1_o, kd1_o, b1_o,
                   *, tm, tiles_per_seq):
    xe = jnp.concatenate([xp_ref[...], x_ref[...], xn_ref[...]], axis=0)
    he = jnp.where(_edge_keep_mask(tm, tiles_per_seq), _rms(xe, g_ref[...]), 0.0)
    h_prev, h, h_next = _neighbours(he, tm)
    xx = 0.5 * (h_prev + h_next) - h

    def mix(n):
        return (h + xx * mu_ref[n:n + 1, :]).astype(BF16)

    r = _dot(mix(0), wr_ref[...])
    k = _dot(mix(2), wk_ref[...])
    v = _dot(mix(3), wv_ref[...])
    gate_o[...] = _dot(jax.nn.sigmoid(_dot(mix(5), g1_ref[...])).astype(BF16), g2_ref[...])
    w_lora = jnp.tanh(_dot(mix(1), w1_ref[...])).astype(BF16)
    a_lora = _dot(mix(4), a1_ref[...]).astype(BF16)

    ones = ones_ref[...]
    kk_raw = k * kk_ref[...]
    kk = kk_raw / jnp.maximum(jnp.sqrt(_head_sum(kk_raw * kk_raw, ones)), L2_EPS)
    r_o[...] = r
    v_o[...] = v
    kk_o[...] = kk

    kd_sum = None
    for e, (lw_o, kd_o, b_o) in enumerate(((lw0_o, kd0_o, b0_o), (lw1_o, kd1_o, b1_o))):
        w_pre = _dot(w_lora, w2_ref[e]) + w0_ref[e:e + 1, :]
        lw_o[...] = -math.exp(-0.5) * jax.nn.sigmoid(w_pre)
        a = jax.nn.sigmoid(_dot(a_lora, a2_ref[e]) + a0_ref[e:e + 1, :])
        kd = k * (1.0 + (a - 1.0) * ka_ref[...])
        kd_o[...] = kd
        b_o[...] = kk * a
        kd_sum = kd if kd_sum is None else kd_sum + kd
    bonus_o[...] = _head_sum(r * kd_sum * rk_ref[...], ones) * v


def _tm_pre(x, seq_len, g, p):
    rows = x.shape[0]
    tm = TM_TM
    row_spec = pl.BlockSpec((tm, D_MODEL), lambda i: (i, 0))
    prev_spec, next_spec = _halo_specs(tm, rows)
    consts = (g, p["mu"], p["w_r"], p["w_k"], p["w_v"], p["g1"], p["g2"], p["w1"], p["w2"], p["w0"],
              p["a1"], p["a2"], p["a0"], p["k_k"], p["k_a"], p["r_k"], p["ones"])
    return pl.pallas_call(
        functools.partial(_tm_pre_kernel, tm=tm, tiles_per_seq=seq_len // tm),
        grid=(rows // tm,),
        in_specs=[prev_spec, row_spec, next_spec] + [_const_spec(c.shape) for c in consts],
        out_specs=[row_spec] * 11,
        out_shape=[jax.ShapeDtypeStruct(x.shape, F32)] * 11,
        compiler_params=_params(),
        name="rwkv7_projections",
    )(x, x, x, *consts)


def _block_diag(x, left):
    zero = jnp.zeros_like(x)
    return jnp.concatenate([jnp.where(left, x, zero), jnp.where(left, zero, x)], axis=0)


def _scan_kernel(r_ref, v_ref, kk_ref, lw_ref, kd_ref, b_ref, y_ref,
                 state_ref, rq_ref, mp_ref, g_ref, decay_ref, *, rows, reverse):
    L = SCAN_L
    n_chunks = rows // L
    pairs = range(N_PAIRS)
    lanes = [slice(p * PAIR, (p + 1) * PAIR) for p in pairs]

    @pl.when(pl.program_id(1) == 0)
    def _():
        state_ref[...] = jnp.zeros_like(state_ref)

    t_idx = lax.broadcasted_iota(jnp.int32, (L, PAIR), 0)
    lane = lax.broadcasted_iota(jnp.int32, (L, PAIR), 1)
    s_idx = lane % L
    left = lane < HEAD
    if reverse:
        strict, incl = s_idx > t_idx, s_idx >= t_idx
    else:
        strict, incl = s_idx < t_idx, s_idx <= t_idx
    eye = jnp.where(s_idx == t_idx, 1.0, 0.0)
    left_state = lax.broadcasted_iota(jnp.int32, (HEAD, PAIR), 1) < HEAD
    sq_row = lax.broadcasted_iota(jnp.int32, (PAIR, PAIR), 0) < HEAD
    sq_col = lax.broadcasted_iota(jnp.int32, (PAIR, PAIR), 1) < HEAD
    same_head = sq_row == sq_col
    t_full = lax.broadcasted_iota(jnp.int32, (L, D_MODEL), 0)
    last = 0 if reverse else L - 1

    def bd(x):
        return _block_diag(x, left)

    def chunk_operands(c):
        rs = pl.ds(pl.multiple_of(c * L, L), L)
        lw = lw_ref[rs, :]
        cum = lw
        for s in (1, 2, 4, 8, 16, 32):
            if reverse:
                cum = cum + jnp.where(t_full < L - s, pltpu.roll(cum, L - s, axis=0), 0.0)
            else:
                cum = cum + jnp.where(t_full >= s, pltpu.roll(cum, s, axis=0), 0.0)
        half = 0.5 * cum[last:last + 1, :]
        e_half = jnp.exp(half)
        decay_ref[c] = e_half * e_half
        q_kappa = kk_ref[rs, :] * jnp.exp(cum - lw - half)
        q_r = r_ref[rs, :] * jnp.exp(cum - half)
        e_neg = jnp.exp(half - cum)
        b_t = b_ref[rs, :] * e_neg
        k_t = kd_ref[rs, :] * e_neg
        return dict(
            rs=rs, c=c,
            q_in=jnp.concatenate([q_kappa, q_r], axis=0).astype(BF16),
            bt=b_t.astype(BF16), kt=k_t.astype(BF16), v=v_ref[rs, :].astype(BF16),
            kf=(q_kappa * e_half).astype(BF16),
            r_full=q_r * e_half,
            bend=(b_t * e_half).astype(BF16),
            kend=(k_t * e_half).astype(BF16))

    def prepare(c2, carry):
        ops = [chunk_operands(c2 * PREP_CHUNKS + i) for i in range(PREP_CHUNKS)]
        probs = [(o, s) for o in ops for s in lanes]
        n = range(len(probs))
        a_all = [_dot_nt(o["q_in"][:, s], jnp.concatenate([bd(o["bt"][:, s]), bd(o["kt"][:, s])], axis=0))
                 for o, s in probs]
        n_mat = [jnp.where(strict, a[:L, :PAIR], 0.0) for a in a_all]
        a_low = [jnp.concatenate([jnp.where(strict, a[:L, PAIR:], 0.0), jnp.where(incl, a[L:, PAIR:], 0.0)],
                                 axis=0).astype(BF16) for a in a_all]
        a_rb = [jnp.where(incl, a[L:, :PAIR], 0.0).astype(BF16) for a in a_all]
        av = [_dot(a_low[i], bd(probs[i][0]["v"][:, probs[i][1]])) for i in n]
        t_inv = [eye - x for x in n_mat]
        qb = [(-x).astype(BF16) for x in n_mat]
        q = [_dot(qb[i], bd(qb[i])) for i in n]
        for _ in range(int(math.log2(L)) - 2):
            qb = [x.astype(BF16) for x in q]
            tq = [_dot(jnp.concatenate([t_inv[i].astype(BF16), qb[i]], axis=0), bd(qb[i])) for i in n]
            t_inv = [t_inv[i] + tq[i][:L] for i in n]
            q = [tq[i][L:] for i in n]
        t_inv = [t_inv[i] + _dot(t_inv[i].astype(BF16), bd(q[i].astype(BF16))) for i in n]
        tx = [_dot(t_inv[i].astype(BF16),
                   jnp.concatenate([bd(probs[i][0]["kf"][:, probs[i][1]]), bd(av[i][:L].astype(BF16))], axis=1))
              for i in n]
        kft16 = [x[:, :PAIR].astype(BF16) for x in tx]
        u016 = [(-x[:, PAIR:]).astype(BF16) for x in tx]
        ry = [_dot(a_rb[i], jnp.concatenate([bd(kft16[i]), bd(u016[i])], axis=1)) for i in n]
        zeros = jnp.zeros((L, PAIR), BF16)
        mg = [_dot_tn(jnp.concatenate([jnp.concatenate([kft16[i], u016[i]], axis=1),
                                       jnp.concatenate([zeros, probs[i][0]["v"][:, probs[i][1]]], axis=1)], axis=0),
                      jnp.concatenate([probs[i][0]["bend"][:, probs[i][1]], probs[i][0]["kend"][:, probs[i][1]]],
                                      axis=0)) for i in n]
        for i in n:
            o, s = probs[i]
            p = i % N_PAIRS
            rq_ref[o["c"], :, s] = (o["r_full"][:, s] - ry[i][:, :PAIR]).astype(BF16)
            y_ref[o["rs"], s] = ry[i][:, PAIR:] + av[i][L:]
            mp_ref[o["c"], p] = jnp.where(same_head, -mg[i][:PAIR], 0.0).astype(BF16)
            g_ref[o["c"], p] = jnp.where(left_state, mg[i][PAIR:PAIR + HEAD], mg[i][PAIR + HEAD:])
        return carry

    lax.fori_loop(0, n_chunks // PREP_CHUNKS, prepare, 0)

    state = [state_ref[p] for p in pairs]
    for c in range(n_chunks):
        cc = (n_chunks - 1 - c) if reverse else c
        rs = slice(cc * L, (cc + 1) * L)
        decay = decay_ref[cc]
        sb = [x.astype(BF16) for x in state]
        carried = [_dot(sb[p], mp_ref[cc, p]) for p in pairs]
        from_state = [_dot_nt(rq_ref[cc, :, lanes[p]], _block_diag(sb[p], left_state)) for p in pairs]
        state = [state[p] * decay[:, lanes[p]] + carried[p] + g_ref[cc, p] for p in pairs]
        for p in pairs:
            y_ref[rs, lanes[p]] = y_ref[rs, lanes[p]] + from_state[p]
    for p in pairs:
        state_ref[p] = state[p]


def _scan(r, v, kk, lw, kd, b, batch, seq_len, reverse):
    rows = SCAN_ROWS
    nblk = seq_len // rows

    def idx(bi, j):
        return (bi * nblk + ((nblk - 1 - j) if reverse else j), 0)

    spec = pl.BlockSpec((rows, D_MODEL), idx)
    return pl.pallas_call(
        functools.partial(_scan_kernel, rows=rows, reverse=reverse),
        grid=(batch, nblk),
        in_specs=[spec] * 6,
        out_specs=spec,
        out_shape=jax.ShapeDtypeStruct(r.shape, F32),
        scratch_shapes=[pltpu.VMEM((N_PAIRS, HEAD, PAIR), F32),
                        pltpu.VMEM((rows // SCAN_L, SCAN_L, D_MODEL), BF16),
                        pltpu.VMEM((rows // SCAN_L, N_PAIRS, PAIR, PAIR), BF16),
                        pltpu.VMEM((rows // SCAN_L, N_PAIRS, HEAD, PAIR), F32),
                        pltpu.VMEM((rows // SCAN_L, 1, D_MODEL), F32)],
        compiler_params=_params(2),
        name="rwkv7_scan_bwd" if reverse else "rwkv7_scan_fwd",
    )(r, v, kk, lw, kd, b)


def _tm_post_kernel(x_ref, yf_ref, yb_ref, bonus_ref, gate_ref, lng_ref, lnb_ref, ones_ref, wo_ref, o_ref):
    ones = ones_ref[...]
    y = yf_ref[...] + yb_ref[...]
    d = y - _head_sum(y, ones) * (1.0 / HEAD)
    var = _head_sum(d * d, ones) * (1.0 / HEAD)
    yn = d * lax.rsqrt(var + GN_EPS) * lng_ref[...] + lnb_ref[...]
    out = ((yn + bonus_ref[...]) * gate_ref[...]).astype(BF16)
    o_ref[...] = x_ref[...] + _dot(out, wo_ref[...])


def _tm_post(x, yf, yb, bonus, gate, p):
    rows = x.shape[0]
    tm = TM_TM
    row_spec = pl.BlockSpec((tm, D_MODEL), lambda i: (i, 0))
    consts = (p["ln_g"], p["ln_b"], p["ones"], p["w_o"])
    return pl.pallas_call(
        _tm_post_kernel,
        grid=(rows // tm,),
        in_specs=[row_spec] * 5 + [_const_spec(c.shape) for c in consts],
        out_specs=row_spec,
        out_shape=jax.ShapeDtypeStruct(x.shape, F32),
        compiler_params=_params(),
        name="rwkv7_output",
    )(x, yf, yb, bonus, gate, *consts)


def _row(v):
    return v.reshape(1, -1).astype(F32)


def _lora_out_padded(w2):
    z = jnp.zeros_like(w2[0])
    return jnp.stack([jnp.concatenate([w2[0], z], axis=0), jnp.concatenate([z, w2[1]], axis=0)]).astype(BF16)


def _prepare(norm_mix_g, norm_ffn_g, final_norm_g,
             sg_w_in, sg_b_in, sg_norm_g, sg_w_s, sg_b_s, sg_w_out,
             tm_mu, tm_w_r, tm_w_k, tm_w_v, tm_w0, tm_w1, tm_w2, tm_a0, tm_a1, tm_a2,
             tm_g1, tm_g2, tm_k_k, tm_k_a, tm_r_k, tm_ln_g, tm_ln_b, tm_w_o,
             ff_w_gate, ff_w_up, ff_conv_w, ff_conv_b, ff_w_down):
    head_id = jnp.arange(D_MODEL) // HEAD
    ones = (head_id[:, None] == head_id[None, :]).astype(BF16)
    sg, tm, ff = [], [], []
    for j in range(sg_w_in.shape[0]):
        sg.append(dict(
            w_in=sg_w_in[j].astype(BF16), b_in=_row(sg_b_in[j]), norm_g=_row(sg_norm_g[j]),
            w_s=sg_w_s[j].astype(BF16),
            b_full=jnp.repeat(sg_b_s[j].T, SG_GROUP_DIM, axis=1).astype(F32),
            w_out=sg_w_out[j].astype(BF16)))
    for j in range(tm_w_r.shape[0]):
        tm.append(dict(
            mu=tm_mu[j], w_r=tm_w_r[j].astype(BF16), w_k=tm_w_k[j].astype(BF16), w_v=tm_w_v[j].astype(BF16),
            g1=tm_g1[j].astype(BF16), g2=tm_g2[j].astype(BF16),
            w1=jnp.concatenate([tm_w1[j, 0], tm_w1[j, 1]], axis=1).astype(BF16), w2=_lora_out_padded(tm_w2[j]),
            w0=tm_w0[j],
            a1=jnp.concatenate([tm_a1[j, 0], tm_a1[j, 1]], axis=1).astype(BF16), a2=_lora_out_padded(tm_a2[j]),
            a0=tm_a0[j],
            k_k=_row(tm_k_k[j]), k_a=_row(tm_k_a[j]), r_k=_row(tm_r_k[j]),
            ln_g=_row(tm_ln_g[j]), ln_b=_row(tm_ln_b[j]), w_o=tm_w_o[j].astype(BF16), ones=ones))
    for i in range(ff_w_gate.shape[0]):
        ff.append(dict(w_gate=ff_w_gate[i].astype(BF16), w_up=ff_w_up[i].astype(BF16), conv_w=ff_conv_w[i],
                       conv_b=_row(ff_conv_b[i]), w_down=ff_w_down[i].astype(BF16)))
    return dict(mix_g=norm_mix_g, ffn_g=norm_ffn_g, final_g=_row(final_norm_g), sg=sg, tm=tm, ff=ff)


def _trunk(x3, prm):
    batch, seq_len, _ = x3.shape
    x = x3.reshape(batch * seq_len, D_MODEL)
    depth = prm["mix_g"].shape[0]
    for i in range(depth):
        g = _row(prm["mix_g"][i])
        if i % 2 == 0:
            p = prm["sg"][i // 2]
            x = _sg_layer(x, g, p["w_in"], p["b_in"], p["norm_g"], p["w_s"], p["b_full"], p["w_out"])
        else:
            p = prm["tm"][i // 2]
            r, v, kk, gate, bonus, lw0, kd0, b0, lw1, kd1, b1 = _tm_pre(x, seq_len, g, p)
            yf = _scan(r, v, kk, lw0, kd0, b0, batch, seq_len, False)
            yb = _scan(r, v, kk, lw1, kd1, b1, batch, seq_len, True)
            x = _tm_post(x, yf, yb, bonus, gate, p)
        f = prm["ff"][i]
        x = _ffn_layer(x, seq_len, _row(prm["ffn_g"][i]), f["w_gate"], f["w_up"], f["conv_w"], f["conv_b"],
                       f["w_down"], prm["final_g"], final=(i == depth - 1))
    return x.reshape(batch, seq_len, D_MODEL)


def kernel(x_prompt, x_sample, norm_mix_g, norm_ffn_g, final_norm_g, sg_w_in, sg_b_in, sg_norm_g, sg_w_s, sg_b_s, sg_w_out, tm_mu, tm_w_r, tm_w_k, tm_w_v, tm_w0, tm_w1, tm_w2, tm_a0, tm_a1, tm_a2, tm_g1, tm_g2, tm_k_k, tm_k_a, tm_r_k, tm_ln_g, tm_ln_b, tm_w_o, ff_w_gate, ff_w_up, ff_conv_w, ff_conv_b, ff_w_down):
    prm = _prepare(norm_mix_g, norm_ffn_g, final_norm_g, sg_w_in, sg_b_in, sg_norm_g, sg_w_s, sg_b_s, sg_w_out,
                   tm_mu, tm_w_r, tm_w_k, tm_w_v, tm_w0, tm_w1, tm_w2, tm_a0, tm_a1, tm_a2,
                   tm_g1, tm_g2, tm_k_k, tm_k_a, tm_r_k, tm_ln_g, tm_ln_b, tm_w_o,
                   ff_w_gate, ff_w_up, ff_conv_w, ff_conv_b, ff_w_down)
    return (_trunk(x_prompt, prm), _trunk(x_sample, prm))
```

```python
import functools
import math

import jax
import jax.numpy as jnp
from jax import lax
from jax.experimental import pallas as pl
from jax.experimental.pallas import tpu as pltpu

F32 = jnp.float32
BF16 = jnp.bfloat16

D_MODEL = 1024
SG_CHUNK = 128
SG_HALF = 2 * D_MODEL
SG_GROUPS = 8
SG_GROUP_DIM = SG_HALF // SG_GROUPS
HEAD = 64
PAIR = 2 * HEAD
N_PAIRS = D_MODEL // PAIR
LORA_W = 64
LORA_A = 64
GN_EPS = 64e-5
RMS_EPS = 1e-6
L2_EPS = 1e-12
HALO = 8
SCAN_L = 64
VMEM_LIMIT = 56 * 1024 * 1024

TM_SG = 512
TM_FFN = 512
TM_TM = 512
TM_POST = 256
SCAN_ROWS = 256


def _rms(x, g):
    return x * lax.rsqrt(jnp.mean(x * x, axis=-1, keepdims=True) + RMS_EPS) * g


def _gelu(x):
    return 0.5 * x * (1.0 + lax.erf(x * (1.0 / math.sqrt(2.0))))


def _dot(a, b):
    return jnp.dot(a, b, preferred_element_type=F32)


def _dot_nt(a, b):
    return lax.dot_general(a, b, (((1,), (1,)), ((), ())), preferred_element_type=F32)


def _dot_tn(a, b):
    return lax.dot_general(a, b, (((0,), (0,)), ((), ())), preferred_element_type=F32)


def _head_sum(a, gather, scatter):
    return _dot(_dot(a.astype(BF16), gather).astype(BF16), scatter)


def _const_spec(shape):
    nd = len(shape)
    return pl.BlockSpec(shape, lambda *_: (0,) * nd, pipeline_mode=pl.Buffered(1))


def _params(n_axes=1):
    return pltpu.CompilerParams(dimension_semantics=("arbitrary",) * n_axes,
                                vmem_limit_bytes=VMEM_LIMIT)


def _halo_specs(tm, total_rows):
    per = tm // HALO
    last = total_rows // HALO - 1
    prev = pl.BlockSpec((HALO, D_MODEL), lambda i: (jnp.maximum(i * per - 1, 0), 0))
    nxt = pl.BlockSpec((HALO, D_MODEL), lambda i: (jnp.minimum((i + 1) * per, last), 0))
    return prev, nxt


def _edge_keep_mask(tm, tiles_per_seq):
    i = pl.program_id(0)
    pos = i % tiles_per_seq
    row = lax.broadcasted_iota(jnp.int32, (tm + 2 * HALO, 1), 0)
    drop = jnp.logical_or(jnp.logical_and(pos == 0, row < HALO),
                          jnp.logical_and(pos == tiles_per_seq - 1, row >= tm + HALO))
    return jnp.logical_not(drop)


def _neighbours(ext, tm):
    n = tm + 2 * HALO
    prev = pltpu.roll(ext, 1, axis=0)[HALO:HALO + tm]
    nxt = pltpu.roll(ext, n - 1, axis=0)[HALO:HALO + tm]
    return prev, ext[HALO:HALO + tm], nxt


def _sg_kernel(x_ref, g_ref, win_ref, bin_ref, ng_ref, ws_ref, bs_ref, wout_ref, o_ref, uv_ref, *, tm):
    x = x_ref[...]
    h = _rms(x, g_ref[...]).astype(BF16)
    z = _gelu(_dot(h, win_ref[...]) + bin_ref[...])
    u = z[:, :SG_HALF]
    v = _rms(z[:, SG_HALF:], ng_ref[...]).astype(BF16)
    for c in range(tm // SG_CHUNK):
        rows = slice(c * SG_CHUNK, (c + 1) * SG_CHUNK)
        for g in range(SG_GROUPS):
            cols = slice(g * SG_GROUP_DIM, (g + 1) * SG_GROUP_DIM)
            mixed = _dot(ws_ref[g], v[rows, cols]) + bs_ref[:, cols]
            uv_ref[rows, cols] = (u[rows, cols] * mixed).astype(BF16)
    o_ref[...] = x + _dot(uv_ref[...], wout_ref[...])


def _sg_layer(x, g, w_in, b_in, norm_g, w_s, b_full, w_out):
    rows = x.shape[0]
    tm = TM_SG
    row_spec = pl.BlockSpec((tm, D_MODEL), lambda i: (i, 0))
    return pl.pallas_call(
        functools.partial(_sg_kernel, tm=tm),
        grid=(rows // tm,),
        in_specs=[row_spec, _const_spec(g.shape), _const_spec(w_in.shape), _const_spec(b_in.shape),
                  _const_spec(norm_g.shape), _const_spec(w_s.shape), _const_spec(b_full.shape),
                  _const_spec(w_out.shape)],
        out_specs=row_spec,
        out_shape=jax.ShapeDtypeStruct(x.shape, F32),
        scratch_shapes=[pltpu.VMEM((tm, SG_HALF), BF16)],
        compiler_params=_params(),
        name="sg_mixer",
    )(x, g, w_in, b_in, norm_g, w_s, b_full, w_out)


def _ffn_kernel(xp_ref, x_ref, xn_ref, g_ref, wg_ref, wu_ref, cw_ref, cb_ref, wd_ref, fg_ref, o_ref,
                *, tm, tiles_per_seq, final):
    x = x_ref[...]
    xe = jnp.concatenate([xp_ref[...], x, xn_ref[...]], axis=0)
    he = _rms(xe, g_ref[...]).astype(BF16)
    gate = jnp.where(_edge_keep_mask(tm, tiles_per_seq), _dot(he, wg_ref[...]), 0.0)
    g_prev, g_mid, g_next = _neighbours(gate, tm)
    conv = g_prev * cw_ref[0:1, :] + g_mid * cw_ref[1:2, :] + g_next * cw_ref[2:3, :] + cb_ref[...]
    up = _dot(he[HALO:HALO + tm], wu_ref[...])
    hh = (_gelu(conv) * up).astype(BF16)
    y = x + _dot(hh, wd_ref[...])
    if final:
        y = _rms(y, fg_ref[...])
    o_ref[...] = y


def _ffn_layer(x, seq_len, g, w_gate, w_up, conv_w, conv_b, w_down, final_g, final):
    rows = x.shape[0]
    tm = TM_FFN
    row_spec = pl.BlockSpec((tm, D_MODEL), lambda i: (i, 0))
    prev_spec, next_spec = _halo_specs(tm, rows)
    return pl.pallas_call(
        functools.partial(_ffn_kernel, tm=tm, tiles_per_seq=seq_len // tm, final=final),
        grid=(rows // tm,),
        in_specs=[prev_spec, row_spec, next_spec, _const_spec(g.shape), _const_spec(w_gate.shape),
                  _const_spec(w_up.shape), _const_spec(conv_w.shape), _const_spec(conv_b.shape),
                  _const_spec(w_down.shape), _const_spec(final_g.shape)],
        out_specs=row_spec,
        out_shape=jax.ShapeDtypeStruct(x.shape, F32),
        compiler_params=_params(),
        name="conv_glu_ffn",
    )(x, x, x, g, w_gate, w_up, conv_w, conv_b, w_down, final_g)


def _tm_pre_kernel(xp_ref, x_ref, xn_ref, g_ref, mu_ref, wr_ref, wk_ref, wv_ref, g1_ref, g2_ref,
                   w1_ref, w2_ref, w0_ref, a1_ref, a2_ref, a0_ref, kk_ref, ka_ref, rk_ref, gather_ref, scatter_ref,
                   r_o, v_o, kk_o, gate_o, bonus_o, lw0_o, kd0_o, b0_o, lw1_o, kd1_o, b1_o,
                   *, tm, tiles_per_seq):
    xe = jnp.concatenate([xp_ref[...], x_ref[...], xn_ref[...]], axis=0)
    he = jnp.where(_edge_keep_mask(tm, tiles_per_seq), _rms(xe, g_ref[...]), 0.0)
    h_prev, h, h_next = _neighbours(he, tm)
    xx = 0.5 * (h_prev + h_next) - h
    gather, scatter = gather_ref[...], scatter_ref[...]

    def mix(n):
        return (h + xx * mu_ref[n:n + 1, :]).astype(BF16)

    r = _dot(mix(0), wr_ref[...])
    k = _dot(mix(2), wk_ref[...])
    v = _dot(mix(3), wv_ref[...])
    gate_o[...] = _dot(jax.nn.sigmoid(_dot(mix(5), g1_ref[...])).astype(BF16), g2_ref[...]).astype(BF16)
    w_lora = jnp.tanh(_dot(mix(1), w1_ref[...])).astype(BF16)
    a_lora = _dot(mix(4), a1_ref[...]).astype(BF16)

    kk_raw = k * kk_ref[...]
    kk = kk_raw / jnp.maximum(jnp.sqrt(_head_sum(kk_raw * kk_raw, gather, scatter)), L2_EPS)
    r_o[...] = r.astype(BF16)
    v_o[...] = v.astype(BF16)
    kk_o[...] = kk.astype(BF16)

    kd_sum = None
    for e, (lw_o, kd_o, b_o) in enumerate(((lw0_o, kd0_o, b0_o), (lw1_o, kd1_o, b1_o))):
        w_pre = _dot(w_lora, w2_ref[e]) + w0_ref[e:e + 1, :]
        lw_o[...] = -math.exp(-0.5) * jax.nn.sigmoid(w_pre)
        a = jax.nn.sigmoid(_dot(a_lora, a2_ref[e]) + a0_ref[e:e + 1, :])
        kd = k * (1.0 + (a - 1.0) * ka_ref[...])
        kd_o[...] = kd.astype(BF16)
        b_o[...] = (kk * a).astype(BF16)
        kd_sum = kd if kd_sum is None else kd_sum + kd
    bonus_o[...] = (_head_sum(r * kd_sum * rk_ref[...], gather, scatter) * v).astype(BF16)


def _tm_pre(x, seq_len, g, p):
    rows = x.shape[0]
    tm = TM_TM
    row_spec = pl.BlockSpec((tm, D_MODEL), lambda i: (i, 0))
    prev_spec, next_spec = _halo_specs(tm, rows)
    consts = (g, p["mu"], p["w_r"], p["w_k"], p["w_v"], p["g1"], p["g2"], p["w1"], p["w2"], p["w0"],
              p["a1"], p["a2"], p["a0"], p["k_k"], p["k_a"], p["r_k"], p["gather"], p["scatter"])
    half = jax.ShapeDtypeStruct(x.shape, BF16)
    full = jax.ShapeDtypeStruct(x.shape, F32)
    return pl.pallas_call(
        functools.partial(_tm_pre_kernel, tm=tm, tiles_per_seq=seq_len // tm),
        grid=(rows // tm,),
        in_specs=[prev_spec, row_spec, next_spec] + [_const_spec(c.shape) for c in consts],
        out_specs=[row_spec] * 11,
        out_shape=[half, half, half, half, half, full, half, half, full, half, half],
        compiler_params=_params(),
        name="rwkv7_projections",
    )(x, x, x, *consts)


def _block_diag(x, left):
    zero = jnp.zeros_like(x)
    return jnp.concatenate([jnp.where(left, x, zero), jnp.where(left, zero, x)], axis=0)


def _scan_kernel(r_ref, v_ref, kk_ref, lw_ref, kd_ref, b_ref, y_ref,
                 state_ref, rq_ref, mp_ref, g_ref, decay_ref, *, rows, reverse):
    L = SCAN_L
    n_chunks = rows // L
    pairs = range(N_PAIRS)
    lanes = [slice(p * PAIR, (p + 1) * PAIR) for p in pairs]

    @pl.when(pl.program_id(1) == 0)
    def _():
        state_ref[...] = jnp.zeros_like(state_ref)

    t_idx = lax.broadcasted_iota(jnp.int32, (L, PAIR), 0)
    lane = lax.broadcasted_iota(jnp.int32, (L, PAIR), 1)
    s_idx = lane % L
    left = lane < HEAD
    if reverse:
        strict, incl = s_idx > t_idx, s_idx >= t_idx
    else:
        strict, incl = s_idx < t_idx, s_idx <= t_idx
    eye = jnp.where(s_idx == t_idx, 1.0, 0.0)
    left_state = lax.broadcasted_iota(jnp.int32, (HEAD, PAIR), 1) < HEAD
    sq_row = lax.broadcasted_iota(jnp.int32, (PAIR, PAIR), 0) < HEAD
    sq_col = lax.broadcasted_iota(jnp.int32, (PAIR, PAIR), 1) < HEAD
    same_head = sq_row == sq_col
    t_full = lax.broadcasted_iota(jnp.int32, (L, D_MODEL), 0)
    last = 0 if reverse else L - 1

    def bd(x):
        return _block_diag(x, left)

    def chunk_operands(c):
        rs = slice(c * L, (c + 1) * L)
        lw = lw_ref[rs, :]
        cum = lw
        for s in (1, 2, 4, 8, 16, 32):
            if reverse:
                cum = cum + jnp.where(t_full < L - s, pltpu.roll(cum, L - s, axis=0), 0.0)
            else:
                cum = cum + jnp.where(t_full >= s, pltpu.roll(cum, s, axis=0), 0.0)
        half = 0.5 * cum[last:last + 1, :]
        e_half = jnp.exp(half)
        decay_ref[c] = e_half * e_half
        q_kappa = kk_ref[rs, :].astype(F32) * jnp.exp(cum - lw - half)
        q_r = r_ref[rs, :].astype(F32) * jnp.exp(cum - half)
        e_neg = jnp.exp(half - cum)
        b_t = b_ref[rs, :].astype(F32) * e_neg
        k_t = kd_ref[rs, :].astype(F32) * e_neg
        return dict(
            rs=rs, c=c,
            q_in=jnp.concatenate([q_kappa, q_r], axis=0).astype(BF16),
            bt=b_t.astype(BF16), kt=k_t.astype(BF16), v=v_ref[rs, :],
            kf=(q_kappa * e_half).astype(BF16),
            r_full=q_r * e_half,
            bend=(b_t * e_half).astype(BF16),
            kend=(k_t * e_half).astype(BF16))

    def prepare():
        ops = [chunk_operands(c) for c in range(n_chunks)]
        probs = [(o, s) for o in ops for s in lanes]
        n = range(len(probs))
        a_all = [_dot_nt(o["q_in"][:, s], jnp.concatenate([bd(o["bt"][:, s]), bd(o["kt"][:, s])], axis=0))
                 for o, s in probs]
        n_mat = [jnp.where(strict, a[:L, :PAIR], 0.0) for a in a_all]
        a_low = [jnp.concatenate([jnp.where(strict, a[:L, PAIR:], 0.0), jnp.where(incl, a[L:, PAIR:], 0.0)],
                                 axis=0).astype(BF16) for a in a_all]
        a_rb = [jnp.where(incl, a[L:, :PAIR], 0.0).astype(BF16) for a in a_all]
        av = [_dot(a_low[i], bd(probs[i][0]["v"][:, probs[i][1]])) for i in n]
        t_inv = [eye - x for x in n_mat]
        qb = [(-x).astype(BF16) for x in n_mat]
        q = [_dot(qb[i], bd(qb[i])) for i in n]
        for _ in range(int(math.log2(L)) - 2):
            qb = [x.astype(BF16) for x in q]
            tq = [_dot(jnp.concatenate([t_inv[i].astype(BF16), qb[i]], axis=0), bd(qb[i])) for i in n]
            t_inv = [t_inv[i] + tq[i][:L] for i in n]
            q = [tq[i][L:] for i in n]
        t_inv = [t_inv[i] + _dot(t_inv[i].astype(BF16), bd(q[i].astype(BF16))) for i in n]
        tx = [_dot(t_inv[i].astype(BF16),
                   jnp.concatenate([bd(probs[i][0]["kf"][:, probs[i][1]]), bd(av[i][:L].astype(BF16))], axis=1))
              for i in n]
        kft16 = [x[:, :PAIR].astype(BF16) for x in tx]
        u016 = [(-x[:, PAIR:]).astype(BF16) for x in tx]
        ry = [_dot(a_rb[i], jnp.concatenate([bd(kft16[i]), bd(u016[i])], axis=1)) for i in n]
        zeros = jnp.zeros((L, PAIR), BF16)
        mg = [_dot_tn(jnp.concatenate([jnp.concatenate([kft16[i], u016[i]], axis=1),
                                       jnp.concatenate([zeros, probs[i][0]["v"][:, probs[i][1]]], axis=1)], axis=0),
                      jnp.concatenate([probs[i][0]["bend"][:, probs[i][1]], probs[i][0]["kend"][:, probs[i][1]]],
                                      axis=0)) for i in n]
        for i in n:
            o, s = probs[i]
            p = i % N_PAIRS
            rq_ref[o["c"], :, s] = (o["r_full"][:, s] - ry[i][:, :PAIR]).astype(BF16)
            y_ref[o["rs"], s] = ry[i][:, PAIR:] + av[i][L:]
            mp_ref[o["c"], p] = jnp.where(same_head, -mg[i][:PAIR], 0.0).astype(BF16)
            g_ref[o["c"], p] = jnp.where(left_state, mg[i][PAIR:PAIR + HEAD], mg[i][PAIR + HEAD:])

    prepare()

    state = [state_ref[p] for p in pairs]
    for c in range(n_chunks):
        cc = (n_chunks - 1 - c) if reverse else c
        rs = slice(cc * L, (cc + 1) * L)
        decay = decay_ref[cc]
        sb = [x.astype(BF16) for x in state]
        carried = [_dot(sb[p], mp_ref[cc, p]) for p in pairs]
        from_state = [_dot_nt(rq_ref[cc, :, lanes[p]], _block_diag(sb[p], left_state)) for p in pairs]
        state = [state[p] * decay[:, lanes[p]] + carried[p] + g_ref[cc, p] for p in pairs]
        for p in pairs:
            y_ref[rs, lanes[p]] = y_ref[rs, lanes[p]] + from_state[p]
    for p in pairs:
        state_ref[p] = state[p]


def _scan(r, v, kk, lw, kd, b, batch, seq_len, reverse):
    rows = SCAN_ROWS
    nblk = seq_len // rows
    n_chunks = rows // SCAN_L

    def idx(bi, j):
        return (bi * nblk + ((nblk - 1 - j) if reverse else j), 0)

    spec = pl.BlockSpec((rows, D_MODEL), idx)
    return pl.pallas_call(
        functools.partial(_scan_kernel, rows=rows, reverse=reverse),
        grid=(batch, nblk),
        in_specs=[spec] * 6,
        out_specs=spec,
        out_shape=jax.ShapeDtypeStruct(r.shape, F32),
        scratch_shapes=[pltpu.VMEM((N_PAIRS, HEAD, PAIR), F32),
                        pltpu.VMEM((n_chunks, SCAN_L, D_MODEL), BF16),
                        pltpu.VMEM((n_chunks, N_PAIRS, PAIR, PAIR), BF16),
                        pltpu.VMEM((n_chunks, N_PAIRS, HEAD, PAIR), F32),
                        pltpu.VMEM((n_chunks, 1, D_MODEL), F32)],
        compiler_params=_params(2),
        name="rwkv7_scan_bwd" if reverse else "rwkv7_scan_fwd",
    )(r, v, kk, lw, kd, b)


def _tm_post_kernel(x_ref, yf_ref, yb_ref, bonus_ref, gate_ref, lng_ref, lnb_ref, gather_ref, scatter_ref,
                    wo_ref, o_ref):
    gather, scatter = gather_ref[...], scatter_ref[...]
    y = yf_ref[...] + yb_ref[...]
    d = y - _head_sum(y, gather, scatter) * (1.0 / HEAD)
    var = _head_sum(d * d, gather, scatter) * (1.0 / HEAD)
    yn = d * lax.rsqrt(var + GN_EPS) * lng_ref[...] + lnb_ref[...]
    out = ((yn + bonus_ref[...].astype(F32)) * gate_ref[...].astype(F32)).astype(BF16)
    o_ref[...] = x_ref[...] + _dot(out, wo_ref[...])


def _tm_post(x, yf, yb, bonus, gate, p):
    rows = x.shape[0]
    tm = TM_POST
    row_spec = pl.BlockSpec((tm, D_MODEL), lambda i: (i, 0))
    consts = (p["ln_g"], p["ln_b"], p["gather"], p["scatter"], p["w_o"])
    return pl.pallas_call(
        _tm_post_kernel,
        grid=(rows // tm,),
        in_specs=[row_spec] * 5 + [_const_spec(c.shape) for c in consts],
        out_specs=row_spec,
        out_shape=jax.ShapeDtypeStruct(x.shape, F32),
        compiler_params=_params(),
        name="rwkv7_output",
    )(x, yf, yb, bonus, gate, *consts)


def _row(v):
    return v.reshape(1, -1).astype(F32)


def _lora_out_padded(w2):
    z = jnp.zeros_like(w2[0])
    return jnp.stack([jnp.concatenate([w2[0], z], axis=0), jnp.concatenate([z, w2[1]], axis=0)]).astype(BF16)


def _prepare(norm_mix_g, norm_ffn_g, final_norm_g,
             sg_w_in, sg_b_in, sg_norm_g, sg_w_s, sg_b_s, sg_w_out,
             tm_mu, tm_w_r, tm_w_k, tm_w_v, tm_w0, tm_w1, tm_w2, tm_a0, tm_a1, tm_a2,
             tm_g1, tm_g2, tm_k_k, tm_k_a, tm_r_k, tm_ln_g, tm_ln_b, tm_w_o,
             ff_w_gate, ff_w_up, ff_conv_w, ff_conv_b, ff_w_down):
    head_id = jnp.arange(D_MODEL) // HEAD
    gather = (head_id[:, None] == jnp.arange(PAIR)[None, :]).astype(BF16)
    scatter = gather.T
    sg, tm, ff = [], [], []
    for j in range(sg_w_in.shape[0]):
        sg.append(dict(
            w_in=sg_w_in[j].astype(BF16), b_in=_row(sg_b_in[j]), norm_g=_row(sg_norm_g[j]),
            w_s=sg_w_s[j].astype(BF16),
            b_full=jnp.repeat(sg_b_s[j].T, SG_GROUP_DIM, axis=1).astype(F32),
            w_out=sg_w_out[j].astype(BF16)))
    for j in range(tm_w_r.shape[0]):
        tm.append(dict(
            mu=tm_mu[j], w_r=tm_w_r[j].astype(BF16), w_k=tm_w_k[j].astype(BF16), w_v=tm_w_v[j].astype(BF16),
            g1=tm_g1[j].astype(BF16), g2=tm_g2[j].astype(BF16),
            w1=jnp.concatenate([tm_w1[j, 0], tm_w1[j, 1]], axis=1).astype(BF16), w2=_lora_out_padded(tm_w2[j]),
            w0=tm_w0[j],
            a1=jnp.concatenate([tm_a1[j, 0], tm_a1[j, 1]], axis=1).astype(BF16), a2=_lora_out_padded(tm_a2[j]),
            a0=tm_a0[j],
            k_k=_row(tm_k_k[j]), k_a=_row(tm_k_a[j]), r_k=_row(tm_r_k[j]),
            ln_g=_row(tm_ln_g[j]), ln_b=_row(tm_ln_b[j]), w_o=tm_w_o[j].astype(BF16), gather=gather, scatter=scatter))
    for i in range(ff_w_gate.shape[0]):
        ff.append(dict(w_gate=ff_w_gate[i].astype(BF16), w_up=ff_w_up[i].astype(BF16), conv_w=ff_conv_w[i],
                       conv_b=_row(ff_conv_b[i]), w_down=ff_w_down[i].astype(BF16)))
    return dict(mix_g=norm_mix_g, ffn_g=norm_ffn_g, final_g=_row(final_norm_g), sg=sg, tm=tm, ff=ff)


def _trunk(x3, prm):
    batch, seq_len, _ = x3.shape
    x = x3.reshape(batch * seq_len, D_MODEL)
    depth = prm["mix_g"].shape[0]
    for i in range(depth):
        g = _row(prm["mix_g"][i])
        if i % 2 == 0:
            p = prm["sg"][i // 2]
            x = _sg_layer(x, g, p["w_in"], p["b_in"], p["norm_g"], p["w_s"], p["b_full"], p["w_out"])
        else:
            p = prm["tm"][i // 2]
            r, v, kk, gate, bonus, lw0, kd0, b0, lw1, kd1, b1 = _tm_pre(x, seq_len, g, p)
            yf = _scan(r, v, kk, lw0, kd0, b0, batch, seq_len, False)
            yb = _scan(r, v, kk, lw1, kd1, b1, batch, seq_len, True)
            x = _tm_post(x, yf, yb, bonus, gate, p)
        f = prm["ff"][i]
        x = _ffn_layer(x, seq_len, _row(prm["ffn_g"][i]), f["w_gate"], f["w_up"], f["conv_w"], f["conv_b"],
                       f["w_down"], prm["final_g"], final=(i == depth - 1))
    return x.reshape(batch, seq_len, D_MODEL)


def kernel(x_prompt, x_sample, norm_mix_g, norm_ffn_g, final_norm_g, sg_w_in, sg_b_in, sg_norm_g, sg_w_s, sg_b_s, sg_w_out, tm_mu, tm_w_r, tm_w_k, tm_w_v, tm_w0, tm_w1, tm_w2, tm_a0, tm_a1, tm_a2, tm_g1, tm_g2, tm_k_k, tm_k_a, tm_r_k, tm_ln_g, tm_ln_b, tm_w_o, ff_w_gate, ff_w_up, ff_conv_w, ff_conv_b, ff_w_down):
    prm = _prepare(norm_mix_g, norm_ffn_g, final_norm_g, sg_w_in, sg_b_in, sg_norm_g, sg_w_s, sg_b_s, sg_w_out,
                   tm_mu, tm_w_r, tm_w_k, tm_w_v, tm_w0, tm_w1, tm_w2, tm_a0, tm_a1, tm_a2,
                   tm_g1, tm_g2, tm_k_k, tm_k_a, tm_r_k, tm_ln_g, tm_ln_b, tm_w_o,
                   ff_w_gate, ff_w_up, ff_conv_w, ff_conv_b, ff_w_down)
    return (_trunk(x_prompt, prm), _trunk(x_sample, prm))
```

```python
import functools
import math

import jax
import jax.numpy as jnp
from jax import lax
from jax.experimental import pallas as pl
from jax.experimental.pallas import tpu as pltpu

F32 = jnp.float32
BF16 = jnp.bfloat16

D_MODEL = 1024
SG_CHUNK = 128
SG_HALF = 2 * D_MODEL
SG_GROUPS = 8
SG_GROUP_DIM = SG_HALF // SG_GROUPS
HEAD = 64
PAIR = 2 * HEAD
N_PAIRS = D_MODEL // PAIR
LORA_W = 64
LORA_A = 64
GN_EPS = 64e-5
RMS_EPS = 1e-6
L2_EPS = 1e-12
HALO = 16
SCAN_L = 64
VMEM_LIMIT = 56 * 1024 * 1024

TM_SG = 512
TM_FFN = 512
TM_FFN_FUSED = 512
TM_TM = 512
SCAN_ROWS = 256


def _rms(x, g):
    return x * lax.rsqrt(jnp.mean(x * x, axis=-1, keepdims=True) + RMS_EPS) * g


def _gelu(x):
    return 0.5 * x * (1.0 + lax.erf(x * (1.0 / math.sqrt(2.0))))


def _dot(a, b):
    return jnp.dot(a, b, preferred_element_type=F32)


def _dot_nt(a, b):
    return lax.dot_general(a, b, (((1,), (1,)), ((), ())), preferred_element_type=F32)


def _dot_tn(a, b):
    return lax.dot_general(a, b, (((0,), (0,)), ((), ())), preferred_element_type=F32)


def _head_sum(a, gather, scatter):
    return _dot(_dot(a.astype(BF16), gather).astype(BF16), scatter)


def _const_spec(shape):
    nd = len(shape)
    return pl.BlockSpec(shape, lambda *_: (0,) * nd, pipeline_mode=pl.Buffered(1))


def _params(n_axes=1):
    return pltpu.CompilerParams(dimension_semantics=("arbitrary",) * n_axes,
                                vmem_limit_bytes=VMEM_LIMIT)


def _halo_specs(tm, total_rows):
    per = tm // HALO
    last = total_rows // HALO - 1
    prev = pl.BlockSpec((HALO, D_MODEL), lambda i: (jnp.maximum(i * per - 1, 0), 0))
    nxt = pl.BlockSpec((HALO, D_MODEL), lambda i: (jnp.minimum((i + 1) * per, last), 0))
    return prev, nxt


def _edge_keep_mask(tm, tiles_per_seq):
    i = pl.program_id(0)
    pos = i % tiles_per_seq
    row = lax.broadcasted_iota(jnp.int32, (tm + 2 * HALO, 1), 0)
    drop = jnp.logical_or(jnp.logical_and(pos == 0, row < HALO),
                          jnp.logical_and(pos == tiles_per_seq - 1, row >= tm + HALO))
    return jnp.logical_not(drop)


def _neighbours(ext, tm):
    n = tm + 2 * HALO
    prev = pltpu.roll(ext, 1, axis=0)[HALO:HALO + tm]
    nxt = pltpu.roll(ext, n - 1, axis=0)[HALO:HALO + tm]
    return prev, ext[HALO:HALO + tm], nxt


def _sg_kernel(x_ref, g_ref, win_ref, bin_ref, ng_ref, ws_ref, bs_ref, wout_ref, o_ref, uv_ref, *, tm):
    x = x_ref[...]
    h = _rms(x, g_ref[...]).astype(BF16)
    z = _gelu(_dot(h, win_ref[...]) + bin_ref[...])
    u = z[:, :SG_HALF]
    v = _rms(z[:, SG_HALF:], ng_ref[...]).astype(BF16)
    for c in range(tm // SG_CHUNK):
        rows = slice(c * SG_CHUNK, (c + 1) * SG_CHUNK)
        for g in range(SG_GROUPS):
            cols = slice(g * SG_GROUP_DIM, (g + 1) * SG_GROUP_DIM)
            mixed = _dot(ws_ref[g], v[rows, cols]) + bs_ref[:, cols]
            uv_ref[rows, cols] = (u[rows, cols] * mixed).astype(BF16)
    o_ref[...] = x + _dot(uv_ref[...], wout_ref[...])


def _sg_layer(x, g, w_in, b_in, norm_g, w_s, b_full, w_out):
    rows = x.shape[0]
    tm = TM_SG
    row_spec = pl.BlockSpec((tm, D_MODEL), lambda i: (i, 0))
    return pl.pallas_call(
        functools.partial(_sg_kernel, tm=tm),
        grid=(rows // tm,),
        in_specs=[row_spec, _const_spec(g.shape), _const_spec(w_in.shape), _const_spec(b_in.shape),
                  _const_spec(norm_g.shape), _const_spec(w_s.shape), _const_spec(b_full.shape),
                  _const_spec(w_out.shape)],
        out_specs=row_spec,
        out_shape=jax.ShapeDtypeStruct(x.shape, F32),
        scratch_shapes=[pltpu.VMEM((tm, SG_HALF), BF16)],
        compiler_params=_params(),
        name="sg_mixer",
    )(x, g, w_in, b_in, norm_g, w_s, b_full, w_out)


def _rwkv_out(y, bonus, gate, lng_ref, lnb_ref, gather_ref, scatter_ref, wo_ref):
    gather, scatter = gather_ref[...], scatter_ref[...]
    d = y - _head_sum(y, gather, scatter) * (1.0 / HEAD)
    var = _head_sum(d * d, gather, scatter) * (1.0 / HEAD)
    yn = d * lax.rsqrt(var + GN_EPS) * lng_ref[...] + lnb_ref[...]
    out = ((yn + bonus.astype(F32)) * gate.astype(F32)).astype(BF16)
    return _dot(out, wo_ref[...])


def _ffn_kernel(*refs, tm, tiles_per_seq, final, fuse_rwkv_out):
    def ext(triple):
        return jnp.concatenate([t[...] for t in triple], axis=0)

    if fuse_rwkv_out:
        x3, y3, bonus3, gate3 = refs[0:3], refs[3:6], refs[6:9], refs[9:12]
        lng_ref, lnb_ref, gather_ref, scatter_ref, wo_ref = refs[12:17]
        g_ref, wg_ref, wu_ref, cw_ref, cb_ref, wd_ref, fg_ref, o_ref = refs[17:]
        xe = ext(x3) + _rwkv_out(ext(y3), ext(bonus3), ext(gate3), lng_ref, lnb_ref, gather_ref, scatter_ref, wo_ref)
    else:
        g_ref, wg_ref, wu_ref, cw_ref, cb_ref, wd_ref, fg_ref, o_ref = refs[3:]
        xe = ext(refs[0:3])
    x = xe[HALO:HALO + tm]
    he = _rms(xe, g_ref[...]).astype(BF16)
    gate = jnp.where(_edge_keep_mask(tm, tiles_per_seq), _dot(he, wg_ref[...]), 0.0)
    g_prev, g_mid, g_next = _neighbours(gate, tm)
    conv = g_prev * cw_ref[0:1, :] + g_mid * cw_ref[1:2, :] + g_next * cw_ref[2:3, :] + cb_ref[...]
    up = _dot(he[HALO:HALO + tm], wu_ref[...])
    hh = (_gelu(conv) * up).astype(BF16)
    y = x + _dot(hh, wd_ref[...])
    if final:
        y = _rms(y, fg_ref[...])
    o_ref[...] = y


def _ffn_layer(x, seq_len, g, w_gate, w_up, conv_w, conv_b, w_down, final_g, final, rwkv_out=None):
    rows = x.shape[0]
    tm = TM_FFN_FUSED if rwkv_out else TM_FFN
    row_spec = pl.BlockSpec((tm, D_MODEL), lambda i: (i, 0))
    prev_spec, next_spec = _halo_specs(tm, rows)
    streams, consts = [x], []
    if rwkv_out:
        y, bonus, gate, p = rwkv_out
        streams += [y, bonus, gate]
        consts += [p["ln_g"], p["ln_b"], p["gather"], p["scatter"], p["w_o"]]
    consts += [g, w_gate, w_up, conv_w, conv_b, w_down, final_g]
    return pl.pallas_call(
        functools.partial(_ffn_kernel, tm=tm, tiles_per_seq=seq_len // tm, final=final,
                          fuse_rwkv_out=bool(rwkv_out)),
        grid=(rows // tm,),
        in_specs=[prev_spec, row_spec, next_spec] * len(streams) + [_const_spec(c.shape) for c in consts],
        out_specs=row_spec,
        out_shape=jax.ShapeDtypeStruct(x.shape, F32),
        compiler_params=_params(),
        name="rwkv7_out_conv_glu_ffn" if rwkv_out else "conv_glu_ffn",
    )(*[s for s in streams for _ in range(3)], *consts)


def _tm_pre_kernel(xp_ref, x_ref, xn_ref, g_ref, mu_ref, wr_ref, wk_ref, wv_ref, g1_ref, g2_ref,
                   w1_ref, w2_ref, w0_ref, a1_ref, a2_ref, a0_ref, kk_ref, ka_ref, rk_ref, gather_ref, scatter_ref,
                   r_o, v_o, kk_o, gate_o, bonus_o, lw0_o, kd0_o, b0_o, lw1_o, kd1_o, b1_o,
                   *, tm, tiles_per_seq):
    xe = jnp.concatenate([xp_ref[...], x_ref[...], xn_ref[...]], axis=0)
    he = jnp.where(_edge_keep_mask(tm, tiles_per_seq), _rms(xe, g_ref[...]), 0.0)
    h_prev, h, h_next = _neighbours(he, tm)
    xx = 0.5 * (h_prev + h_next) - h
    gather, scatter = gather_ref[...], scatter_ref[...]

    def mix(n):
        return (h + xx * mu_ref[n:n + 1, :]).astype(BF16)

    r = _dot(mix(0), wr_ref[...])
    k = _dot(mix(2), wk_ref[...])
    v = _dot(mix(3), wv_ref[...])
    gate_o[...] = _dot(jax.nn.sigmoid(_dot(mix(5), g1_ref[...])).astype(BF16), g2_ref[...]).astype(BF16)
    w_lora = jnp.tanh(_dot(mix(1), w1_ref[...])).astype(BF16)
    a_lora = _dot(mix(4), a1_ref[...]).astype(BF16)

    kk_raw = k * kk_ref[...]
    kk = kk_raw / jnp.maximum(jnp.sqrt(_head_sum(kk_raw * kk_raw, gather, scatter)), L2_EPS)
    r_o[...] = r.astype(BF16)
    v_o[...] = v.astype(BF16)
    kk_o[...] = kk.astype(BF16)

    kd_sum = None
    for e, (lw_o, kd_o, b_o) in enumerate(((lw0_o, kd0_o, b0_o), (lw1_o, kd1_o, b1_o))):
        w_pre = _dot(w_lora, w2_ref[e]) + w0_ref[e:e + 1, :]
        lw_o[...] = -math.exp(-0.5) * jax.nn.sigmoid(w_pre)
        a = jax.nn.sigmoid(_dot(a_lora, a2_ref[e]) + a0_ref[e:e + 1, :])
        kd = k * (1.0 + (a - 1.0) * ka_ref[...])
        kd_o[...] = kd.astype(BF16)
        b_o[...] = (kk * a).astype(BF16)
        kd_sum = kd if kd_sum is None else kd_sum + kd
    bonus_o[...] = (_head_sum(r * kd_sum * rk_ref[...], gather, scatter) * v).astype(BF16)


def _tm_pre(x, seq_len, g, p):
    rows = x.shape[0]
    tm = TM_TM
    row_spec = pl.BlockSpec((tm, D_MODEL), lambda i: (i, 0))
    prev_spec, next_spec = _halo_specs(tm, rows)
    consts = (g, p["mu"], p["w_r"], p["w_k"], p["w_v"], p["g1"], p["g2"], p["w1"], p["w2"], p["w0"],
              p["a1"], p["a2"], p["a0"], p["k_k"], p["k_a"], p["r_k"], p["gather"], p["scatter"])
    half = jax.ShapeDtypeStruct(x.shape, BF16)
    full = jax.ShapeDtypeStruct(x.shape, F32)
    return pl.pallas_call(
        functools.partial(_tm_pre_kernel, tm=tm, tiles_per_seq=seq_len // tm),
        grid=(rows // tm,),
        in_specs=[prev_spec, row_spec, next_spec] + [_const_spec(c.shape) for c in consts],
        out_specs=[row_spec] * 11,
        out_shape=[half, half, half, half, half, full, half, half, full, half, half],
        compiler_params=_params(),
        name="rwkv7_projections",
    )(x, x, x, *consts)


def _block_diag(x, left):
    zero = jnp.zeros_like(x)
    return jnp.concatenate([jnp.where(left, x, zero), jnp.where(left, zero, x)], axis=0)


def _scan_kernel(*refs, rows, reverse, accumulate):
    r_ref, v_ref, kk_ref, lw_ref, kd_ref, b_ref = refs[:6]
    acc_ref = refs[6] if accumulate else None
    y_ref, state_ref, rq_ref, mp_ref, g_ref, decay_ref = refs[-6:]
    _scan_body(r_ref, v_ref, kk_ref, lw_ref, kd_ref, b_ref, acc_ref, y_ref,
               state_ref, rq_ref, mp_ref, g_ref, decay_ref, rows=rows, reverse=reverse)


def _scan_body(r_ref, v_ref, kk_ref, lw_ref, kd_ref, b_ref, acc_ref, y_ref,
               state_ref, rq_ref, mp_ref, g_ref, decay_ref, *, rows, reverse):
    L = SCAN_L
    n_chunks = rows // L
    pairs = range(N_PAIRS)
    lanes = [slice(p * PAIR, (p + 1) * PAIR) for p in pairs]

    @pl.when(pl.program_id(1) == 0)
    def _():
        state_ref[...] = jnp.zeros_like(state_ref)

    t_idx = lax.broadcasted_iota(jnp.int32, (L, PAIR), 0)
    lane = lax.broadcasted_iota(jnp.int32, (L, PAIR), 1)
    s_idx = lane % L
    left = lane < HEAD
    if reverse:
        strict, incl = s_idx > t_idx, s_idx >= t_idx
    else:
        strict, incl = s_idx < t_idx, s_idx <= t_idx
    eye = jnp.where(s_idx == t_idx, 1.0, 0.0)
    left_state = lax.broadcasted_iota(jnp.int32, (HEAD, PAIR), 1) < HEAD
    sq_row = lax.broadcasted_iota(jnp.int32, (PAIR, PAIR), 0) < HEAD
    sq_col = lax.broadcasted_iota(jnp.int32, (PAIR, PAIR), 1) < HEAD
    same_head = sq_row == sq_col
    t_full = lax.broadcasted_iota(jnp.int32, (L, D_MODEL), 0)
    last = 0 if reverse else L - 1

    def bd(x):
        return _block_diag(x, left)

    def chunk_operands(c):
        rs = slice(c * L, (c + 1) * L)
        lw = lw_ref[rs, :]
        cum = lw
        for s in (1, 2, 4, 8, 16, 32):
            if reverse:
                cum = cum + jnp.where(t_full < L - s, pltpu.roll(cum, L - s, axis=0), 0.0)
            else:
                cum = cum + jnp.where(t_full >= s, pltpu.roll(cum, s, axis=0), 0.0)
        half = 0.5 * cum[last:last + 1, :]
        e_half = jnp.exp(half)
        decay_ref[c] = e_half * e_half
        q_kappa = kk_ref[rs, :].astype(F32) * jnp.exp(cum - lw - half)
        q_r = r_ref[rs, :].astype(F32) * jnp.exp(cum - half)
        e_neg = jnp.exp(half - cum)
        b_t = b_ref[rs, :].astype(F32) * e_neg
        k_t = kd_ref[rs, :].astype(F32) * e_neg
        return dict(
            rs=rs, c=c,
            q_in=jnp.concatenate([q_kappa, q_r], axis=0).astype(BF16),
            bt=b_t.astype(BF16), kt=k_t.astype(BF16), v=v_ref[rs, :],
            kf=(q_kappa * e_half).astype(BF16),
            r_full=q_r * e_half,
            bend=(b_t * e_half).astype(BF16),
            kend=(k_t * e_half).astype(BF16))

    def prepare():
        ops = [chunk_operands(c) for c in range(n_chunks)]
        probs = [(o, s) for o in ops for s in lanes]
        n = range(len(probs))
        a_all = [_dot_nt(o["q_in"][:, s], jnp.concatenate([bd(o["bt"][:, s]), bd(o["kt"][:, s])], axis=0))
                 for o, s in probs]
        n_mat = [jnp.where(strict, a[:L, :PAIR], 0.0) for a in a_all]
        a_low = [jnp.concatenate([jnp.where(strict, a[:L, PAIR:], 0.0), jnp.where(incl, a[L:, PAIR:], 0.0)],
                                 axis=0).astype(BF16) for a in a_all]
        a_rb = [jnp.where(incl, a[L:, :PAIR], 0.0).astype(BF16) for a in a_all]
        av = [_dot(a_low[i], bd(probs[i][0]["v"][:, probs[i][1]])) for i in n]
        t_inv = [eye - x for x in n_mat]
        qb = [(-x).astype(BF16) for x in n_mat]
        q = [_dot(qb[i], bd(qb[i])) for i in n]
        for _ in range(int(math.log2(L)) - 2):
            qb = [x.astype(BF16) for x in q]
            tq = [_dot(jnp.concatenate([t_inv[i].astype(BF16), qb[i]], axis=0), bd(qb[i])) for i in n]
            t_inv = [t_inv[i] + tq[i][:L] for i in n]
            q = [tq[i][L:] for i in n]
        t_inv = [t_inv[i] + _dot(t_inv[i].astype(BF16), bd(q[i].astype(BF16))) for i in n]
        tx = [_dot(t_inv[i].astype(BF16),
                   jnp.concatenate([bd(probs[i][0]["kf"][:, probs[i][1]]), bd(av[i][:L].astype(BF16))], axis=1))
              for i in n]
        kft16 = [x[:, :PAIR].astype(BF16) for x in tx]
        u016 = [(-x[:, PAIR:]).astype(BF16) for x in tx]
        ry = [_dot(a_rb[i], jnp.concatenate([bd(kft16[i]), bd(u016[i])], axis=1)) for i in n]
        zeros = jnp.zeros((L, PAIR), BF16)
        mg = [_dot_tn(jnp.concatenate([jnp.concatenate([kft16[i], u016[i]], axis=1),
                                       jnp.concatenate([zeros, probs[i][0]["v"][:, probs[i][1]]], axis=1)], axis=0),
                      jnp.concatenate([probs[i][0]["bend"][:, probs[i][1]], probs[i][0]["kend"][:, probs[i][1]]],
                                      axis=0)) for i in n]
        for i in n:
            o, s = probs[i]
            p = i % N_PAIRS
            rq_ref[o["c"], :, s] = (o["r_full"][:, s] - ry[i][:, :PAIR]).astype(BF16)
            y0 = ry[i][:, PAIR:] + av[i][L:]
            y_ref[o["rs"], s] = y0 if acc_ref is None else y0 + acc_ref[o["rs"], s]
            mp_ref[o["c"], p] = jnp.where(same_head, -mg[i][:PAIR], 0.0).astype(BF16)
            g_ref[o["c"], p] = jnp.where(left_state, mg[i][PAIR:PAIR + HEAD], mg[i][PAIR + HEAD:])

    prepare()

    state = [state_ref[p] for p in pairs]
    for c in range(n_chunks):
        cc = (n_chunks - 1 - c) if reverse else c
        rs = slice(cc * L, (cc + 1) * L)
        decay = decay_ref[cc]
        sb = [x.astype(BF16) for x in state]
        carried = [_dot(sb[p], mp_ref[cc, p]) for p in pairs]
        from_state = [_dot_nt(rq_ref[cc, :, lanes[p]], _block_diag(sb[p], left_state)) for p in pairs]
        state = [state[p] * decay[:, lanes[p]] + carried[p] + g_ref[cc, p] for p in pairs]
        for p in pairs:
            y_ref[rs, lanes[p]] = y_ref[rs, lanes[p]] + from_state[p]
    for p in pairs:
        state_ref[p] = state[p]


def _scan(r, v, kk, lw, kd, b, acc, batch, seq_len, reverse):
    rows = SCAN_ROWS
    streams = (r, v, kk, lw, kd, b) + (() if acc is None else (acc,))
    nblk = seq_len // rows
    n_chunks = rows // SCAN_L

    def idx(bi, j):
        return (bi * nblk + ((nblk - 1 - j) if reverse else j), 0)

    spec = pl.BlockSpec((rows, D_MODEL), idx)
    return pl.pallas_call(
        functools.partial(_scan_kernel, rows=rows, reverse=reverse, accumulate=acc is not None),
        grid=(batch, nblk),
        in_specs=[spec] * len(streams),
        out_specs=spec,
        out_shape=jax.ShapeDtypeStruct(r.shape, F32),
        scratch_shapes=[pltpu.VMEM((N_PAIRS, HEAD, PAIR), F32),
                        pltpu.VMEM((n_chunks, SCAN_L, D_MODEL), BF16),
                        pltpu.VMEM((n_chunks, N_PAIRS, PAIR, PAIR), BF16),
                        pltpu.VMEM((n_chunks, N_PAIRS, HEAD, PAIR), F32),
                        pltpu.VMEM((n_chunks, 1, D_MODEL), F32)],
        compiler_params=_params(2),
        name="rwkv7_scan_bwd" if reverse else "rwkv7_scan_fwd",
    )(*streams)


def _row(v):
    return v.reshape(1, -1).astype(F32)


def _lora_out_padded(w2):
    z = jnp.zeros_like(w2[0])
    return jnp.stack([jnp.concatenate([w2[0], z], axis=0), jnp.concatenate([z, w2[1]], axis=0)]).astype(BF16)


def _prepare(norm_mix_g, norm_ffn_g, final_norm_g,
             sg_w_in, sg_b_in, sg_norm_g, sg_w_s, sg_b_s, sg_w_out,
             tm_mu, tm_w_r, tm_w_k, tm_w_v, tm_w0, tm_w1, tm_w2, tm_a0, tm_a1, tm_a2,
             tm_g1, tm_g2, tm_k_k, tm_k_a, tm_r_k, tm_ln_g, tm_ln_b, tm_w_o,
             ff_w_gate, ff_w_up, ff_conv_w, ff_conv_b, ff_w_down):
    head_id = jnp.arange(D_MODEL) // HEAD
    gather = (head_id[:, None] == jnp.arange(PAIR)[None, :]).astype(BF16)
    scatter = gather.T
    sg, tm, ff = [], [], []
    for j in range(sg_w_in.shape[0]):
        sg.append(dict(
            w_in=sg_w_in[j].astype(BF16), b_in=_row(sg_b_in[j]), norm_g=_row(sg_norm_g[j]),
            w_s=sg_w_s[j].astype(BF16),
            b_full=jnp.repeat(sg_b_s[j].T, SG_GROUP_DIM, axis=1).astype(F32),
            w_out=sg_w_out[j].astype(BF16)))
    for j in range(tm_w_r.shape[0]):
        tm.append(dict(
            mu=tm_mu[j], w_r=tm_w_r[j].astype(BF16), w_k=tm_w_k[j].astype(BF16), w_v=tm_w_v[j].astype(BF16),
            g1=tm_g1[j].astype(BF16), g2=tm_g2[j].astype(BF16),
            w1=jnp.concatenate([tm_w1[j, 0], tm_w1[j, 1]], axis=1).astype(BF16), w2=_lora_out_padded(tm_w2[j]),
            w0=tm_w0[j],
            a1=jnp.concatenate([tm_a1[j, 0], tm_a1[j, 1]], axis=1).astype(BF16), a2=_lora_out_padded(tm_a2[j]),
            a0=tm_a0[j],
            k_k=_row(tm_k_k[j]), k_a=_row(tm_k_a[j]), r_k=_row(tm_r_k[j]),
            ln_g=_row(tm_ln_g[j]), ln_b=_row(tm_ln_b[j]), w_o=tm_w_o[j].astype(BF16), gather=gather, scatter=scatter))
    for i in range(ff_w_gate.shape[0]):
        ff.append(dict(w_gate=ff_w_gate[i].astype(BF16), w_up=ff_w_up[i].astype(BF16), conv_w=ff_conv_w[i],
                       conv_b=_row(ff_conv_b[i]), w_down=ff_w_down[i].astype(BF16)))
    return dict(mix_g=norm_mix_g, ffn_g=norm_ffn_g, final_g=_row(final_norm_g), sg=sg, tm=tm, ff=ff)


def _trunk(x3, prm):
    batch, seq_len, _ = x3.shape
    x = x3.reshape(batch * seq_len, D_MODEL)
    depth = prm["mix_g"].shape[0]
    for i in range(depth):
        g = _row(prm["mix_g"][i])
        if i % 2 == 0:
            p = prm["sg"][i // 2]
            x = _sg_layer(x, g, p["w_in"], p["b_in"], p["norm_g"], p["w_s"], p["b_full"], p["w_out"])
        else:
            p = prm["tm"][i // 2]
            r, v, kk, gate, bonus, lw0, kd0, b0, lw1, kd1, b1 = _tm_pre(x, seq_len, g, p)
            y = _scan(r, v, kk, lw0, kd0, b0, None, batch, seq_len, False)
            y = _scan(r, v, kk, lw1, kd1, b1, y, batch, seq_len, True)
            rwkv_out = (y, bonus, gate, p)
        f = prm["ff"][i]
        x = _ffn_layer(x, seq_len, _row(prm["ffn_g"][i]), f["w_gate"], f["w_up"], f["conv_w"], f["conv_b"],
                       f["w_down"], prm["final_g"], final=(i == depth - 1),
                       rwkv_out=rwkv_out if i % 2 == 1 else None)
    return x.reshape(batch, seq_len, D_MODEL)


def kernel(x_prompt, x_sample, norm_mix_g, norm_ffn_g, final_norm_g, sg_w_in, sg_b_in, sg_norm_g, sg_w_s, sg_b_s, sg_w_out, tm_mu, tm_w_r, tm_w_k, tm_w_v, tm_w0, tm_w1, tm_w2, tm_a0, tm_a1, tm_a2, tm_g1, tm_g2, tm_k_k, tm_k_a, tm_r_k, tm_ln_g, tm_ln_b, tm_w_o, ff_w_gate, ff_w_up, ff_conv_w, ff_conv_b, ff_w_down):
    prm = _prepare(norm_mix_g, norm_ffn_g, final_norm_g, sg_w_in, sg_b_in, sg_norm_g, sg_w_s, sg_b_s, sg_w_out,
                   tm_mu, tm_w_r, tm_w_k, tm_w_v, tm_w0, tm_w1, tm_w2, tm_a0, tm_a1, tm_a2,
                   tm_g1, tm_g2, tm_k_k, tm_k_a, tm_r_k, tm_ln_g, tm_ln_b, tm_w_o,
                   ff_w_gate, ff_w_up, ff_conv_w, ff_conv_b, ff_w_down)
    return (_trunk(x_prompt, prm), _trunk(x_sample, prm))
```

```python
import functools
import math

import jax
import jax.numpy as jnp
from jax import lax
from jax.experimental import pallas as pl
from jax.experimental.pallas import tpu as pltpu

F32 = jnp.float32
BF16 = jnp.bfloat16

D_MODEL = 1024
SG_CHUNK = 128
SG_HALF = 2 * D_MODEL
SG_GROUPS = 8
SG_GROUP_DIM = SG_HALF // SG_GROUPS
HEAD = 64
PAIR = 2 * HEAD
N_PAIRS = D_MODEL // PAIR
LORA_W = 64
LORA_A = 64
GN_EPS = 64e-5
RMS_EPS = 1e-6
L2_EPS = 1e-12
HALO = 16
SCAN_L = 64
VMEM_LIMIT = 56 * 1024 * 1024

TM_SG = 512
TM_FFN = 512
TM_FFN_FUSED = 512
TM_TM = 512
SCAN_ROWS = 512


def _rms(x, g):
    return x * lax.rsqrt(jnp.mean(x * x, axis=-1, keepdims=True) + RMS_EPS) * g


def _gelu(x):
    return 0.5 * x * (1.0 + lax.erf(x * (1.0 / math.sqrt(2.0))))


def _dot(a, b):
    return jnp.dot(a, b, preferred_element_type=F32)


def _dot_nt(a, b):
    return lax.dot_general(a, b, (((1,), (1,)), ((), ())), preferred_element_type=F32)


def _dot_tn(a, b):
    return lax.dot_general(a, b, (((0,), (0,)), ((), ())), preferred_element_type=F32)


def _head_sum(a, gather, scatter):
    return _dot(_dot(a.astype(BF16), gather).astype(BF16), scatter)


def _const_spec(shape):
    nd = len(shape)
    return pl.BlockSpec(shape, lambda *_: (0,) * nd, pipeline_mode=pl.Buffered(1))


def _params(n_axes=1):
    return pltpu.CompilerParams(dimension_semantics=("arbitrary",) * n_axes,
                                vmem_limit_bytes=VMEM_LIMIT)


def _halo_specs(tm, total_rows):
    per = tm // HALO
    last = total_rows // HALO - 1
    prev = pl.BlockSpec((HALO, D_MODEL), lambda i: (jnp.maximum(i * per - 1, 0), 0))
    nxt = pl.BlockSpec((HALO, D_MODEL), lambda i: (jnp.minimum((i + 1) * per, last), 0))
    return prev, nxt


def _edge_keep_mask(tm, tiles_per_seq):
    i = pl.program_id(0)
    pos = i % tiles_per_seq
    row = lax.broadcasted_iota(jnp.int32, (tm + 2 * HALO, 1), 0)
    drop = jnp.logical_or(jnp.logical_and(pos == 0, row < HALO),
                          jnp.logical_and(pos == tiles_per_seq - 1, row >= tm + HALO))
    return jnp.logical_not(drop)


def _neighbours(ext, tm):
    n = tm + 2 * HALO
    prev = pltpu.roll(ext, 1, axis=0)[HALO:HALO + tm]
    nxt = pltpu.roll(ext, n - 1, axis=0)[HALO:HALO + tm]
    return prev, ext[HALO:HALO + tm], nxt


def _sg_kernel(x_ref, g_ref, win_ref, bin_ref, ng_ref, ws_ref, bs_ref, wout_ref, o_ref, uv_ref, *, tm):
    x = x_ref[...]
    h = _rms(x, g_ref[...]).astype(BF16)
    v = _gelu(_dot(h, win_ref[:, SG_HALF:]) + bin_ref[:, SG_HALF:])
    u = _gelu(_dot(h, win_ref[:, :SG_HALF]) + bin_ref[:, :SG_HALF])
    v = _rms(v, ng_ref[...]).astype(BF16)
    for c in range(tm // SG_CHUNK):
        rows = slice(c * SG_CHUNK, (c + 1) * SG_CHUNK)
        for g in range(SG_GROUPS):
            cols = slice(g * SG_GROUP_DIM, (g + 1) * SG_GROUP_DIM)
            mixed = _dot(ws_ref[g], v[rows, cols]) + bs_ref[:, cols]
            uv_ref[rows, cols] = (u[rows, cols] * mixed).astype(BF16)
    o_ref[...] = x + _dot(uv_ref[...], wout_ref[...])


def _sg_layer(x, g, w_in, b_in, norm_g, w_s, b_full, w_out):
    rows = x.shape[0]
    tm = TM_SG
    row_spec = pl.BlockSpec((tm, D_MODEL), lambda i: (i, 0))
    return pl.pallas_call(
        functools.partial(_sg_kernel, tm=tm),
        grid=(rows // tm,),
        in_specs=[row_spec, _const_spec(g.shape), _const_spec(w_in.shape), _const_spec(b_in.shape),
                  _const_spec(norm_g.shape), _const_spec(w_s.shape), _const_spec(b_full.shape),
                  _const_spec(w_out.shape)],
        out_specs=row_spec,
        out_shape=jax.ShapeDtypeStruct(x.shape, F32),
        scratch_shapes=[pltpu.VMEM((tm, SG_HALF), BF16)],
        compiler_params=_params(),
        name="sg_mixer",
    )(x, g, w_in, b_in, norm_g, w_s, b_full, w_out)


def _rwkv_out(y, bonus, gate, lng_ref, lnb_ref, gather_ref, scatter_ref, wo_ref):
    gather, scatter = gather_ref[...], scatter_ref[...]
    d = y - _head_sum(y, gather, scatter) * (1.0 / HEAD)
    var = _head_sum(d * d, gather, scatter) * (1.0 / HEAD)
    yn = d * lax.rsqrt(var + GN_EPS) * lng_ref[...] + lnb_ref[...]
    out = ((yn + bonus.astype(F32)) * gate.astype(F32)).astype(BF16)
    return _dot(out, wo_ref[...])


def _ffn_kernel(*refs, tm, tiles_per_seq, final, fuse_rwkv_out):
    def ext(triple):
        return jnp.concatenate([t[...] for t in triple], axis=0)

    if fuse_rwkv_out:
        x3, y3, bonus3, gate3 = refs[0:3], refs[3:6], refs[6:9], refs[9:12]
        lng_ref, lnb_ref, gather_ref, scatter_ref, wo_ref = refs[12:17]
        g_ref, wg_ref, wu_ref, cw_ref, cb_ref, wd_ref, fg_ref, o_ref = refs[17:]
        xe = ext(x3) + _rwkv_out(ext(y3), ext(bonus3), ext(gate3), lng_ref, lnb_ref, gather_ref, scatter_ref, wo_ref)
    else:
        g_ref, wg_ref, wu_ref, cw_ref, cb_ref, wd_ref, fg_ref, o_ref = refs[3:]
        xe = ext(refs[0:3])
    x = xe[HALO:HALO + tm]
    he = _rms(xe, g_ref[...]).astype(BF16)
    gate = jnp.where(_edge_keep_mask(tm, tiles_per_seq), _dot(he, wg_ref[...]), 0.0)
    g_prev, g_mid, g_next = _neighbours(gate, tm)
    conv = g_prev * cw_ref[0:1, :] + g_mid * cw_ref[1:2, :] + g_next * cw_ref[2:3, :] + cb_ref[...]
    up = _dot(he[HALO:HALO + tm], wu_ref[...])
    hh = (_gelu(conv) * up).astype(BF16)
    y = x + _dot(hh, wd_ref[...])
    if final:
        y = _rms(y, fg_ref[...])
    o_ref[...] = y


def _ffn_layer(x, seq_len, g, w_gate, w_up, conv_w, conv_b, w_down, final_g, final, rwkv_out=None):
    rows = x.shape[0]
    tm = TM_FFN_FUSED if rwkv_out else TM_FFN
    row_spec = pl.BlockSpec((tm, D_MODEL), lambda i: (i, 0))
    prev_spec, next_spec = _halo_specs(tm, rows)
    streams, consts = [x], []
    if rwkv_out:
        y, bonus, gate, p = rwkv_out
        streams += [y, bonus, gate]
        consts += [p["ln_g"], p["ln_b"], p["gather"], p["scatter"], p["w_o"]]
    consts += [g, w_gate, w_up, conv_w, conv_b, w_down, final_g]
    return pl.pallas_call(
        functools.partial(_ffn_kernel, tm=tm, tiles_per_seq=seq_len // tm, final=final,
                          fuse_rwkv_out=bool(rwkv_out)),
        grid=(rows // tm,),
        in_specs=[prev_spec, row_spec, next_spec] * len(streams) + [_const_spec(c.shape) for c in consts],
        out_specs=row_spec,
        out_shape=jax.ShapeDtypeStruct(x.shape, F32),
        compiler_params=_params(),
        name="rwkv7_out_conv_glu_ffn" if rwkv_out else "conv_glu_ffn",
    )(*[s for s in streams for _ in range(3)], *consts)


def _tm_pre_kernel(xp_ref, x_ref, xn_ref, g_ref, mu_ref, wr_ref, wk_ref, wv_ref, g1_ref, g2_ref,
                   w1_ref, w2_ref, w0_ref, a1_ref, a2_ref, a0_ref, kk_ref, ka_ref, rk_ref, gather_ref, scatter_ref,
                   r_o, v_o, kk_o, gate_o, bonus_o, lw0_o, kd0_o, b0_o, lw1_o, kd1_o, b1_o,
                   *, tm, tiles_per_seq):
    xe = jnp.concatenate([xp_ref[...], x_ref[...], xn_ref[...]], axis=0)
    he = jnp.where(_edge_keep_mask(tm, tiles_per_seq), _rms(xe, g_ref[...]), 0.0)
    h_prev, h, h_next = _neighbours(he, tm)
    xx = 0.5 * (h_prev + h_next) - h
    gather, scatter = gather_ref[...], scatter_ref[...]

    def mix(n):
        return (h + xx * mu_ref[n:n + 1, :]).astype(BF16)

    r = _dot(mix(0), wr_ref[...])
    k = _dot(mix(2), wk_ref[...])
    v = _dot(mix(3), wv_ref[...])
    gate_o[...] = _dot(jax.nn.sigmoid(_dot(mix(5), g1_ref[...])).astype(BF16), g2_ref[...]).astype(BF16)
    w_lora = jnp.tanh(_dot(mix(1), w1_ref[...])).astype(BF16)
    a_lora = _dot(mix(4), a1_ref[...]).astype(BF16)

    kk_raw = k * kk_ref[...]
    kk = kk_raw / jnp.maximum(jnp.sqrt(_head_sum(kk_raw * kk_raw, gather, scatter)), L2_EPS)
    r_o[...] = r.astype(BF16)
    v_o[...] = v.astype(BF16)
    kk_o[...] = kk.astype(BF16)

    kd_sum = None
    for e, (lw_o, kd_o, b_o) in enumerate(((lw0_o, kd0_o, b0_o), (lw1_o, kd1_o, b1_o))):
        w_pre = _dot(w_lora, w2_ref[e]) + w0_ref[e:e + 1, :]
        lw_o[...] = -math.exp(-0.5) * jax.nn.sigmoid(w_pre)
        a = jax.nn.sigmoid(_dot(a_lora, a2_ref[e]) + a0_ref[e:e + 1, :])
        kd = k * (1.0 + (a - 1.0) * ka_ref[...])
        kd_o[...] = kd.astype(BF16)
        b_o[...] = (kk * a).astype(BF16)
        kd_sum = kd if kd_sum is None else kd_sum + kd
    bonus_o[...] = (_head_sum(r * kd_sum * rk_ref[...], gather, scatter) * v).astype(BF16)


def _tm_pre(x, seq_len, g, p):
    rows = x.shape[0]
    tm = TM_TM
    row_spec = pl.BlockSpec((tm, D_MODEL), lambda i: (i, 0))
    prev_spec, next_spec = _halo_specs(tm, rows)
    consts = (g, p["mu"], p["w_r"], p["w_k"], p["w_v"], p["g1"], p["g2"], p["w1"], p["w2"], p["w0"],
              p["a1"], p["a2"], p["a0"], p["k_k"], p["k_a"], p["r_k"], p["gather"], p["scatter"])
    half = jax.ShapeDtypeStruct(x.shape, BF16)
    full = jax.ShapeDtypeStruct(x.shape, F32)
    return pl.pallas_call(
        functools.partial(_tm_pre_kernel, tm=tm, tiles_per_seq=seq_len // tm),
        grid=(rows // tm,),
        in_specs=[prev_spec, row_spec, next_spec] + [_const_spec(c.shape) for c in consts],
        out_specs=[row_spec] * 11,
        out_shape=[half, half, half, half, half, full, half, half, full, half, half],
        compiler_params=_params(),
        name="rwkv7_projections",
    )(x, x, x, *consts)


def _block_diag(x, left):
    zero = jnp.zeros_like(x)
    return jnp.concatenate([jnp.where(left, x, zero), jnp.where(left, zero, x)], axis=0)


def _scan_kernel(*refs, rows, reverse, accumulate):
    r_ref, v_ref, kk_ref, lw_ref, kd_ref, b_ref = refs[:6]
    acc_ref = refs[6] if accumulate else None
    y_ref, state_ref, rq_ref, mp_ref, g_ref, decay_ref = refs[-6:]
    _scan_body(r_ref, v_ref, kk_ref, lw_ref, kd_ref, b_ref, acc_ref, y_ref,
               state_ref, rq_ref, mp_ref, g_ref, decay_ref, rows=rows, reverse=reverse)


def _scan_body(r_ref, v_ref, kk_ref, lw_ref, kd_ref, b_ref, acc_ref, y_ref,
               state_ref, rq_ref, mp_ref, g_ref, decay_ref, *, rows, reverse):
    L = SCAN_L
    n_chunks = rows // L
    pairs = range(N_PAIRS)
    lanes = [slice(p * PAIR, (p + 1) * PAIR) for p in pairs]

    @pl.when(pl.program_id(1) == 0)
    def _():
        state_ref[...] = jnp.zeros_like(state_ref)

    t_idx = lax.broadcasted_iota(jnp.int32, (L, PAIR), 0)
    lane = lax.broadcasted_iota(jnp.int32, (L, PAIR), 1)
    s_idx = lane % L
    left = lane < HEAD
    if reverse:
        strict, incl = s_idx > t_idx, s_idx >= t_idx
    else:
        strict, incl = s_idx < t_idx, s_idx <= t_idx
    eye = jnp.where(s_idx == t_idx, 1.0, 0.0)
    left_state = lax.broadcasted_iota(jnp.int32, (HEAD, PAIR), 1) < HEAD
    sq_row = lax.broadcasted_iota(jnp.int32, (PAIR, PAIR), 0) < HEAD
    sq_col = lax.broadcasted_iota(jnp.int32, (PAIR, PAIR), 1) < HEAD
    same_head = sq_row == sq_col
    t_full = lax.broadcasted_iota(jnp.int32, (L, D_MODEL), 0)
    last = 0 if reverse else L - 1

    def bd(x):
        return _block_diag(x, left)

    def chunk_operands(c):
        rs = slice(c * L, (c + 1) * L)
        lw = lw_ref[rs, :]
        cum = lw
        for s in (1, 2, 4, 8, 16, 32):
            if reverse:
                cum = cum + jnp.where(t_full < L - s, pltpu.roll(cum, L - s, axis=0), 0.0)
            else:
                cum = cum + jnp.where(t_full >= s, pltpu.roll(cum, s, axis=0), 0.0)
        half = 0.5 * cum[last:last + 1, :]
        e_half = jnp.exp(half)
        decay_ref[c] = e_half * e_half
        q_kappa = kk_ref[rs, :].astype(F32) * jnp.exp(cum - lw - half)
        q_r = r_ref[rs, :].astype(F32) * jnp.exp(cum - half)
        e_neg = jnp.exp(half - cum)
        b_t = b_ref[rs, :].astype(F32) * e_neg
        k_t = kd_ref[rs, :].astype(F32) * e_neg
        return dict(
            rs=rs, c=c,
            q_in=jnp.concatenate([q_kappa, q_r], axis=0).astype(BF16),
            bt=b_t.astype(BF16), kt=k_t.astype(BF16), v=v_ref[rs, :],
            kf=(q_kappa * e_half).astype(BF16),
            r_full=q_r * e_half,
            bend=(b_t * e_half).astype(BF16),
            kend=(k_t * e_half).astype(BF16))

    def prepare():
        ops = [chunk_operands(c) for c in range(n_chunks)]
        probs = [(o, s) for o in ops for s in lanes]
        n = range(len(probs))
        a_all = [_dot_nt(o["q_in"][:, s], jnp.concatenate([bd(o["bt"][:, s]), bd(o["kt"][:, s])], axis=0))
                 for o, s in probs]
        n_mat = [jnp.where(strict, a[:L, :PAIR], 0.0) for a in a_all]
        a_low = [jnp.concatenate([jnp.where(strict, a[:L, PAIR:], 0.0), jnp.where(incl, a[L:, PAIR:], 0.0)],
                                 axis=0).astype(BF16) for a in a_all]
        a_rb = [jnp.where(incl, a[L:, :PAIR], 0.0).astype(BF16) for a in a_all]
        av = [_dot(a_low[i], bd(probs[i][0]["v"][:, probs[i][1]])) for i in n]
        t_inv = [eye - x for x in n_mat]
        qb = [(-x).astype(BF16) for x in n_mat]
        q = [_dot(qb[i], bd(qb[i])) for i in n]
        for _ in range(int(math.log2(L)) - 2):
            qb = [x.astype(BF16) for x in q]
            tq = [_dot(jnp.concatenate([t_inv[i].astype(BF16), qb[i]], axis=0), bd(qb[i])) for i in n]
            t_inv = [t_inv[i] + tq[i][:L] for i in n]
            q = [tq[i][L:] for i in n]
        t_inv = [t_inv[i] + _dot(t_inv[i].astype(BF16), bd(q[i].astype(BF16))) for i in n]
        tx = [_dot(t_inv[i].astype(BF16),
                   jnp.concatenate([bd(probs[i][0]["kf"][:, probs[i][1]]), bd(av[i][:L].astype(BF16))], axis=1))
              for i in n]
        kft16 = [x[:, :PAIR].astype(BF16) for x in tx]
        u016 = [(-x[:, PAIR:]).astype(BF16) for x in tx]
        ry = [_dot(a_rb[i], jnp.concatenate([bd(kft16[i]), bd(u016[i])], axis=1)) for i in n]
        zeros = jnp.zeros((L, PAIR), BF16)
        mg = [_dot_tn(jnp.concatenate([jnp.concatenate([kft16[i], u016[i]], axis=1),
                                       jnp.concatenate([zeros, probs[i][0]["v"][:, probs[i][1]]], axis=1)], axis=0),
                      jnp.concatenate([probs[i][0]["bend"][:, probs[i][1]], probs[i][0]["kend"][:, probs[i][1]]],
                                      axis=0)) for i in n]
        for i in n:
            o, s = probs[i]
            p = i % N_PAIRS
            rq_ref[o["c"], :, s] = (o["r_full"][:, s] - ry[i][:, :PAIR]).astype(BF16)
            y0 = ry[i][:, PAIR:] + av[i][L:]
            y_ref[o["rs"], s] = y0 if acc_ref is None else y0 + acc_ref[o["rs"], s]
            mp_ref[o["c"], p] = jnp.where(same_head, -mg[i][:PAIR], 0.0).astype(BF16)
            g_ref[o["c"], p] = jnp.where(left_state, mg[i][PAIR:PAIR + HEAD], mg[i][PAIR + HEAD:])

    prepare()

    state = [state_ref[p] for p in pairs]
    for c in range(n_chunks):
        cc = (n_chunks - 1 - c) if reverse else c
        rs = slice(cc * L, (cc + 1) * L)
        decay = decay_ref[cc]
        sb = [x.astype(BF16) for x in state]
        carried = [_dot(sb[p], mp_ref[cc, p]) for p in pairs]
        from_state = [_dot_nt(rq_ref[cc, :, lanes[p]], _block_diag(sb[p], left_state)) for p in pairs]
        state = [state[p] * decay[:, lanes[p]] + carried[p] + g_ref[cc, p] for p in pairs]
        for p in pairs:
            y_ref[rs, lanes[p]] = y_ref[rs, lanes[p]] + from_state[p]
    for p in pairs:
        state_ref[p] = state[p]


def _scan(r, v, kk, lw, kd, b, acc, batch, seq_len, reverse):
    rows = SCAN_ROWS
    streams = (r, v, kk, lw, kd, b) + (() if acc is None else (acc,))
    nblk = seq_len // rows
    n_chunks = rows // SCAN_L

    def idx(bi, j):
        return (bi * nblk + ((nblk - 1 - j) if reverse else j), 0)

    spec = pl.BlockSpec((rows, D_MODEL), idx)
    return pl.pallas_call(
        functools.partial(_scan_kernel, rows=rows, reverse=reverse, accumulate=acc is not None),
        grid=(batch, nblk),
        in_specs=[spec] * len(streams),
        out_specs=spec,
        out_shape=jax.ShapeDtypeStruct(r.shape, F32),
        scratch_shapes=[pltpu.VMEM((N_PAIRS, HEAD, PAIR), F32),
                        pltpu.VMEM((n_chunks, SCAN_L, D_MODEL), BF16),
                        pltpu.VMEM((n_chunks, N_PAIRS, PAIR, PAIR), BF16),
                        pltpu.VMEM((n_chunks, N_PAIRS, HEAD, PAIR), F32),
                        pltpu.VMEM((n_chunks, 1, D_MODEL), F32)],
        compiler_params=_params(2),
        name="rwkv7_scan_bwd" if reverse else "rwkv7_scan_fwd",
    )(*streams)


def _row(v):
    return v.reshape(1, -1).astype(F32)


def _lora_out_padded(w2):
    z = jnp.zeros_like(w2[0])
    return jnp.stack([jnp.concatenate([w2[0], z], axis=0), jnp.concatenate([z, w2[1]], axis=0)]).astype(BF16)


def _prepare(norm_mix_g, norm_ffn_g, final_norm_g,
             sg_w_in, sg_b_in, sg_norm_g, sg_w_s, sg_b_s, sg_w_out,
             tm_mu, tm_w_r, tm_w_k, tm_w_v, tm_w0, tm_w1, tm_w2, tm_a0, tm_a1, tm_a2,
             tm_g1, tm_g2, tm_k_k, tm_k_a, tm_r_k, tm_ln_g, tm_ln_b, tm_w_o,
             ff_w_gate, ff_w_up, ff_conv_w, ff_conv_b, ff_w_down):
    head_id = jnp.arange(D_MODEL) // HEAD
    gather = (head_id[:, None] == jnp.arange(PAIR)[None, :]).astype(BF16)
    scatter = gather.T
    sg, tm, ff = [], [], []
    for j in range(sg_w_in.shape[0]):
        sg.append(dict(
            w_in=sg_w_in[j].astype(BF16), b_in=_row(sg_b_in[j]), norm_g=_row(sg_norm_g[j]),
            w_s=sg_w_s[j].astype(BF16),
            b_full=jnp.repeat(sg_b_s[j].T, SG_GROUP_DIM, axis=1).astype(F32),
            w_out=sg_w_out[j].astype(BF16)))
    for j in range(tm_w_r.shape[0]):
        tm.append(dict(
            mu=tm_mu[j], w_r=tm_w_r[j].astype(BF16), w_k=tm_w_k[j].astype(BF16), w_v=tm_w_v[j].astype(BF16),
            g1=tm_g1[j].astype(BF16), g2=tm_g2[j].astype(BF16),
            w1=jnp.concatenate([tm_w1[j, 0], tm_w1[j, 1]], axis=1).astype(BF16), w2=_lora_out_padded(tm_w2[j]),
            w0=tm_w0[j],
            a1=jnp.concatenate([tm_a1[j, 0], tm_a1[j, 1]], axis=1).astype(BF16), a2=_lora_out_padded(tm_a2[j]),
            a0=tm_a0[j],
            k_k=_row(tm_k_k[j]), k_a=_row(tm_k_a[j]), r_k=_row(tm_r_k[j]),
            ln_g=_row(tm_ln_g[j]), ln_b=_row(tm_ln_b[j]), w_o=tm_w_o[j].astype(BF16), gather=gather, scatter=scatter))
    for i in range(ff_w_gate.shape[0]):
        ff.append(dict(w_gate=ff_w_gate[i].astype(BF16), w_up=ff_w_up[i].astype(BF16), conv_w=ff_conv_w[i],
                       conv_b=_row(ff_conv_b[i]), w_down=ff_w_down[i].astype(BF16)))
    return dict(mix_g=norm_mix_g, ffn_g=norm_ffn_g, final_g=_row(final_norm_g), sg=sg, tm=tm, ff=ff)


def _trunk(x3, prm):
    batch, seq_len, _ = x3.shape
    x = x3.reshape(batch * seq_len, D_MODEL)
    depth = prm["mix_g"].shape[0]
    for i in range(depth):
        g = _row(prm["mix_g"][i])
        if i % 2 == 0:
            p = prm["sg"][i // 2]
            x = _sg_layer(x, g, p["w_in"], p["b_in"], p["norm_g"], p["w_s"], p["b_full"], p["w_out"])
        else:
            p = prm["tm"][i // 2]
            r, v, kk, gate, bonus, lw0, kd0, b0, lw1, kd1, b1 = _tm_pre(x, seq_len, g, p)
            y = _scan(r, v, kk, lw0, kd0, b0, None, batch, seq_len, False)
            y = _scan(r, v, kk, lw1, kd1, b1, y, batch, seq_len, True)
            rwkv_out = (y, bonus, gate, p)
        f = prm["ff"][i]
        x = _ffn_layer(x, seq_len, _row(prm["ffn_g"][i]), f["w_gate"], f["w_up"], f["conv_w"], f["conv_b"],
                       f["w_down"], prm["final_g"], final=(i == depth - 1),
                       rwkv_out=rwkv_out if i % 2 == 1 else None)
    return x.reshape(batch, seq_len, D_MODEL)


def kernel(x_prompt, x_sample, norm_mix_g, norm_ffn_g, final_norm_g, sg_w_in, sg_b_in, sg_norm_g, sg_w_s, sg_b_s, sg_w_out, tm_mu, tm_w_r, tm_w_k, tm_w_v, tm_w0, tm_w1, tm_w2, tm_a0, tm_a1, tm_a2, tm_g1, tm_g2, tm_k_k, tm_k_a, tm_r_k, tm_ln_g, tm_ln_b, tm_w_o, ff_w_gate, ff_w_up, ff_conv_w, ff_conv_b, ff_w_down):
    prm = _prepare(norm_mix_g, norm_ffn_g, final_norm_g, sg_w_in, sg_b_in, sg_norm_g, sg_w_s, sg_b_s, sg_w_out,
                   tm_mu, tm_w_r, tm_w_k, tm_w_v, tm_w0, tm_w1, tm_w2, tm_a0, tm_a1, tm_a2,
                   tm_g1, tm_g2, tm_k_k, tm_k_a, tm_r_k, tm_ln_g, tm_ln_b, tm_w_o,
                   ff_w_gate, ff_w_up, ff_conv_w, ff_conv_b, ff_w_down)
    return (_trunk(x_prompt, prm), _trunk(x_sample, prm))
```

```python
import functools
import math

import jax
import jax.numpy as jnp
from jax import lax
from jax.experimental import pallas as pl
from jax.experimental.pallas import tpu as pltpu

F32 = jnp.float32
BF16 = jnp.bfloat16

D_MODEL = 1024
SG_CHUNK = 128
SG_HALF = 2 * D_MODEL
SG_GROUPS = 8
SG_GROUP_DIM = SG_HALF // SG_GROUPS
LANES = 128
HEAD = 64
PAIR = 2 * HEAD
assert PAIR == LANES
N_PAIRS = D_MODEL // PAIR
GN_EPS = 64e-5
RMS_EPS = 1e-6
L2_EPS = 1e-12
HALO = 16
SCAN_L = 64
VMEM_LIMIT = 56 * 1024 * 1024

TM_SG = 512
TM_FFN = 512
TM_TM = 512
SCAN_ROWS = 512


def _rms(x, g):
    return x * lax.rsqrt(jnp.mean(x * x, axis=-1, keepdims=True) + RMS_EPS) * g


def _gelu(x):
    return 0.5 * x * (1.0 + lax.erf(x * (1.0 / math.sqrt(2.0))))


def _dot(a, b):
    return jnp.dot(a, b, preferred_element_type=F32)


def _dot_nt(a, b):
    return lax.dot_general(a, b, (((1,), (1,)), ((), ())), preferred_element_type=F32)


def _dot_tn(a, b):
    return lax.dot_general(a, b, (((0,), (0,)), ((), ())), preferred_element_type=F32)


def _head_sum(a, gather, scatter):
    return _dot(_dot(a.astype(BF16), gather).astype(BF16), scatter)


def _const_spec(shape):
    nd = len(shape)
    return pl.BlockSpec(shape, lambda *_: (0,) * nd, pipeline_mode=pl.Buffered(1))


def _params(n_axes=1):
    return pltpu.CompilerParams(dimension_semantics=("arbitrary",) * n_axes,
                                vmem_limit_bytes=VMEM_LIMIT)


def _halo_specs(tm, total_rows):
    per = tm // HALO
    last = total_rows // HALO - 1
    prev = pl.BlockSpec((HALO, D_MODEL), lambda i: (jnp.maximum(i * per - 1, 0), 0))
    nxt = pl.BlockSpec((HALO, D_MODEL), lambda i: (jnp.minimum((i + 1) * per, last), 0))
    return prev, nxt


def _edge_keep_mask(tm, tiles_per_seq):
    i = pl.program_id(0)
    pos = i % tiles_per_seq
    row = lax.broadcasted_iota(jnp.int32, (tm + 2 * HALO, 1), 0)
    drop = jnp.logical_or(jnp.logical_and(pos == 0, row < HALO),
                          jnp.logical_and(pos == tiles_per_seq - 1, row >= tm + HALO))
    return jnp.logical_not(drop)


def _neighbours(ext, tm):
    n = tm + 2 * HALO
    prev = pltpu.roll(ext, 1, axis=0)[HALO:HALO + tm]
    nxt = pltpu.roll(ext, n - 1, axis=0)[HALO:HALO + tm]
    return prev, ext[HALO:HALO + tm], nxt


def _sg_kernel(x_ref, g_ref, win_ref, bin_ref, ng_ref, ws_ref, bs_ref, wout_ref, o_ref, uv_ref, *, tm):
    x = x_ref[...]
    h = _rms(x, g_ref[...]).astype(BF16)
    v = _gelu(_dot(h, win_ref[:, SG_HALF:]) + bin_ref[:, SG_HALF:])
    u = _gelu(_dot(h, win_ref[:, :SG_HALF]) + bin_ref[:, :SG_HALF])
    v = _rms(v, ng_ref[...]).astype(BF16)
    for c in range(tm // SG_CHUNK):
        rows = slice(c * SG_CHUNK, (c + 1) * SG_CHUNK)
        for g in range(SG_GROUPS):
            cols = slice(g * SG_GROUP_DIM, (g + 1) * SG_GROUP_DIM)
            mixed = _dot(ws_ref[g], v[rows, cols]) + bs_ref[:, cols]
            uv_ref[rows, cols] = (u[rows, cols] * mixed).astype(BF16)
    o_ref[...] = x + _dot(uv_ref[...], wout_ref[...])


def _sg_layer(x, g, w_in, b_in, norm_g, w_s, b_full, w_out):
    rows = x.shape[0]
    tm = TM_SG
    row_spec = pl.BlockSpec((tm, D_MODEL), lambda i: (i, 0))
    return pl.pallas_call(
        functools.partial(_sg_kernel, tm=tm),
        grid=(rows // tm,),
        in_specs=[row_spec, _const_spec(g.shape), _const_spec(w_in.shape), _const_spec(b_in.shape),
                  _const_spec(norm_g.shape), _const_spec(w_s.shape), _const_spec(b_full.shape),
                  _const_spec(w_out.shape)],
        out_specs=row_spec,
        out_shape=jax.ShapeDtypeStruct(x.shape, F32),
        scratch_shapes=[pltpu.VMEM((tm, SG_HALF), BF16)],
        compiler_params=_params(),
        name="sg_mixer",
    )(x, g, w_in, b_in, norm_g, w_s, b_full, w_out)


def _rwkv_out(y, bonus, gate, lng_ref, lnb_ref, gather_ref, scatter_ref, wo_ref):
    gather, scatter = gather_ref[...], scatter_ref[...]
    d = y - _head_sum(y, gather, scatter) * (1.0 / HEAD)
    var = _head_sum(d * d, gather, scatter) * (1.0 / HEAD)
    yn = d * lax.rsqrt(var + GN_EPS) * lng_ref[...] + lnb_ref[...]
    out = ((yn + bonus.astype(F32)) * gate.astype(F32)).astype(BF16)
    return _dot(out, wo_ref[...])


def _ffn_kernel(*refs, tm, tiles_per_seq, final, fuse_rwkv_out):
    def ext(triple):
        return jnp.concatenate([t[...] for t in triple], axis=0)

    if fuse_rwkv_out:
        x3, y3, bonus3, gate3 = refs[0:3], refs[3:6], refs[6:9], refs[9:12]
        lng_ref, lnb_ref, gather_ref, scatter_ref, wo_ref = refs[12:17]
        g_ref, wg_ref, wu_ref, cw_ref, cb_ref, wd_ref, fg_ref, o_ref = refs[17:]
        xe = ext(x3) + _rwkv_out(ext(y3), ext(bonus3), ext(gate3), lng_ref, lnb_ref, gather_ref, scatter_ref, wo_ref)
    else:
        g_ref, wg_ref, wu_ref, cw_ref, cb_ref, wd_ref, fg_ref, o_ref = refs[3:]
        xe = ext(refs[0:3])
    x = xe[HALO:HALO + tm]
    he = _rms(xe, g_ref[...]).astype(BF16)
    gate = jnp.where(_edge_keep_mask(tm, tiles_per_seq), _dot(he, wg_ref[...]), 0.0)
    g_prev, g_mid, g_next = _neighbours(gate, tm)
    conv = g_prev * cw_ref[0:1, :] + g_mid * cw_ref[1:2, :] + g_next * cw_ref[2:3, :] + cb_ref[...]
    up = _dot(he[HALO:HALO + tm], wu_ref[...])
    hh = (_gelu(conv) * up).astype(BF16)
    y = x + _dot(hh, wd_ref[...])
    if final:
        y = _rms(y, fg_ref[...])
    o_ref[...] = y


def _ffn_layer(x, seq_len, g, w_gate, w_up, conv_w, conv_b, w_down, final_g, final, rwkv_out=None):
    rows = x.shape[0]
    tm = TM_FFN
    row_spec = pl.BlockSpec((tm, D_MODEL), lambda i: (i, 0))
    prev_spec, next_spec = _halo_specs(tm, rows)
    streams, consts = [x], []
    if rwkv_out:
        y, bonus, gate, p = rwkv_out
        streams += [y, bonus, gate]
        consts += [p["ln_g"], p["ln_b"], p["gather"], p["scatter"], p["w_o"]]
    consts += [g, w_gate, w_up, conv_w, conv_b, w_down, final_g]
    return pl.pallas_call(
        functools.partial(_ffn_kernel, tm=tm, tiles_per_seq=seq_len // tm, final=final,
                          fuse_rwkv_out=bool(rwkv_out)),
        grid=(rows // tm,),
        in_specs=[prev_spec, row_spec, next_spec] * len(streams) + [_const_spec(c.shape) for c in consts],
        out_specs=row_spec,
        out_shape=jax.ShapeDtypeStruct(x.shape, F32),
        compiler_params=_params(),
        name="rwkv7_out_conv_glu_ffn" if rwkv_out else "conv_glu_ffn",
    )(*[s for s in streams for _ in range(3)], *consts)


def _tm_pre_kernel(xp_ref, x_ref, xn_ref, g_ref, mu_ref, wr_ref, wk_ref, wv_ref, g1_ref, g2_ref,
                   w1_ref, w2_ref, w0_ref, a1_ref, a2_ref, a0_ref, kk_ref, ka_ref, rk_ref, gather_ref, scatter_ref,
                   r_o, v_o, kk_o, gate_o, bonus_o, lw0_o, kd0_o, b0_o, lw1_o, kd1_o, b1_o,
                   *, tm, tiles_per_seq):
    xe = jnp.concatenate([xp_ref[...], x_ref[...], xn_ref[...]], axis=0)
    he = jnp.where(_edge_keep_mask(tm, tiles_per_seq), _rms(xe, g_ref[...]), 0.0)
    h_prev, h, h_next = _neighbours(he, tm)
    xx = 0.5 * (h_prev + h_next) - h
    gather, scatter = gather_ref[...], scatter_ref[...]

    def mix(n):
        return (h + xx * mu_ref[n:n + 1, :]).astype(BF16)

    r = _dot(mix(0), wr_ref[...])
    k = _dot(mix(2), wk_ref[...])
    v = _dot(mix(3), wv_ref[...])
    gate_o[...] = _dot(jax.nn.sigmoid(_dot(mix(5), g1_ref[...])).astype(BF16), g2_ref[...]).astype(BF16)
    w_lora = jnp.tanh(_dot(mix(1), w1_ref[...])).astype(BF16)
    a_lora = _dot(mix(4), a1_ref[...]).astype(BF16)

    kk_raw = k * kk_ref[...]
    kk = kk_raw / jnp.maximum(jnp.sqrt(_head_sum(kk_raw * kk_raw, gather, scatter)), L2_EPS)
    r_o[...] = r.astype(BF16)
    v_o[...] = v.astype(BF16)
    kk_o[...] = kk.astype(BF16)

    kd_sum = None
    for e, (lw_o, kd_o, b_o) in enumerate(((lw0_o, kd0_o, b0_o), (lw1_o, kd1_o, b1_o))):
        w_pre = _dot(w_lora, w2_ref[e]) + w0_ref[e:e + 1, :]
        lw_o[...] = -math.exp(-0.5) * jax.nn.sigmoid(w_pre)
        a = jax.nn.sigmoid(_dot(a_lora, a2_ref[e]) + a0_ref[e:e + 1, :])
        kd = k * (1.0 + (a - 1.0) * ka_ref[...])
        kd_o[...] = kd.astype(BF16)
        b_o[...] = (kk * a).astype(BF16)
        kd_sum = kd if kd_sum is None else kd_sum + kd
    bonus_o[...] = (_head_sum(r * kd_sum * rk_ref[...], gather, scatter) * v).astype(BF16)


def _tm_pre(x, seq_len, g, p):
    rows = x.shape[0]
    tm = TM_TM
    row_spec = pl.BlockSpec((tm, D_MODEL), lambda i: (i, 0))
    prev_spec, next_spec = _halo_specs(tm, rows)
    consts = (g, p["mu"], p["w_r"], p["w_k"], p["w_v"], p["g1"], p["g2"], p["w1"], p["w2"], p["w0"],
              p["a1"], p["a2"], p["a0"], p["k_k"], p["k_a"], p["r_k"], p["gather"], p["scatter"])
    half = jax.ShapeDtypeStruct(x.shape, BF16)
    full = jax.ShapeDtypeStruct(x.shape, F32)
    return pl.pallas_call(
        functools.partial(_tm_pre_kernel, tm=tm, tiles_per_seq=seq_len // tm),
        grid=(rows // tm,),
        in_specs=[prev_spec, row_spec, next_spec] + [_const_spec(c.shape) for c in consts],
        out_specs=[row_spec] * 11,
        out_shape=[half, half, half, half, half, full, half, half, full, half, half],
        compiler_params=_params(),
        name="rwkv7_projections",
    )(x, x, x, *consts)


def _block_diag(x, left):
    zero = jnp.zeros_like(x)
    return jnp.concatenate([jnp.where(left, x, zero), jnp.where(left, zero, x)], axis=0)


def _scan_kernel(*refs, rows, reverse, accumulate):
    r_ref, v_ref, kk_ref, lw_ref, kd_ref, b_ref = refs[:6]
    acc_ref = refs[6] if accumulate else None
    y_ref, state_ref, rq_ref, mp_ref, g_ref, decay_ref = refs[-6:]
    _scan_body(r_ref, v_ref, kk_ref, lw_ref, kd_ref, b_ref, acc_ref, y_ref,
               state_ref, rq_ref, mp_ref, g_ref, decay_ref, rows=rows, reverse=reverse)


def _scan_body(r_ref, v_ref, kk_ref, lw_ref, kd_ref, b_ref, acc_ref, y_ref,
               state_ref, rq_ref, mp_ref, g_ref, decay_ref, *, rows, reverse):
    L = SCAN_L
    n_chunks = rows // L
    pairs = range(N_PAIRS)
    lanes = [slice(p * PAIR, (p + 1) * PAIR) for p in pairs]

    @pl.when(pl.program_id(1) == 0)
    def _():
        state_ref[...] = jnp.zeros_like(state_ref)

    t_idx = lax.broadcasted_iota(jnp.int32, (L, PAIR), 0)
    lane = lax.broadcasted_iota(jnp.int32, (L, PAIR), 1)
    s_idx = lane % L
    left = lane < HEAD
    if reverse:
        strict, incl = s_idx > t_idx, s_idx >= t_idx
    else:
        strict, incl = s_idx < t_idx, s_idx <= t_idx
    eye = jnp.where(s_idx == t_idx, 1.0, 0.0)
    left_state = lax.broadcasted_iota(jnp.int32, (HEAD, PAIR), 1) < HEAD
    sq_row = lax.broadcasted_iota(jnp.int32, (PAIR, PAIR), 0) < HEAD
    sq_col = lax.broadcasted_iota(jnp.int32, (PAIR, PAIR), 1) < HEAD
    same_head = sq_row == sq_col
    t_full = lax.broadcasted_iota(jnp.int32, (L, D_MODEL), 0)
    last = 0 if reverse else L - 1

    def bd(x):
        return _block_diag(x, left)

    def chunk_operands(c):
        rs = slice(c * L, (c + 1) * L)
        lw = lw_ref[rs, :]
        cum = lw
        for s in (1 << i for i in range(int(math.log2(L)))):
            if reverse:
                cum = cum + jnp.where(t_full < L - s, pltpu.roll(cum, L - s, axis=0), 0.0)
            else:
                cum = cum + jnp.where(t_full >= s, pltpu.roll(cum, s, axis=0), 0.0)
        half = 0.5 * cum[last:last + 1, :]
        e_half = jnp.exp(half)
        decay_ref[c] = e_half * e_half
        q_kappa = kk_ref[rs, :].astype(F32) * jnp.exp(cum - lw - half)
        q_r = r_ref[rs, :].astype(F32) * jnp.exp(cum - half)
        e_neg = jnp.exp(half - cum)
        b_t = b_ref[rs, :].astype(F32) * e_neg
        k_t = kd_ref[rs, :].astype(F32) * e_neg
        return dict(
            rs=rs, c=c,
            q_in=jnp.concatenate([q_kappa, q_r], axis=0).astype(BF16),
            bt=b_t.astype(BF16), kt=k_t.astype(BF16), v=v_ref[rs, :],
            kf=(q_kappa * e_half).astype(BF16),
            r_full=q_r * e_half,
            bend=(b_t * e_half).astype(BF16),
            kend=(k_t * e_half).astype(BF16))

    def prepare():
        ops = [chunk_operands(c) for c in range(n_chunks)]
        probs = [(o, s) for o in ops for s in lanes]
        n = range(len(probs))
        a_all = [_dot_nt(o["q_in"][:, s], jnp.concatenate([bd(o["bt"][:, s]), bd(o["kt"][:, s])], axis=0))
                 for o, s in probs]
        n_mat = [jnp.where(strict, a[:L, :PAIR], 0.0) for a in a_all]
        a_low = [jnp.concatenate([jnp.where(strict, a[:L, PAIR:], 0.0), jnp.where(incl, a[L:, PAIR:], 0.0)],
                                 axis=0).astype(BF16) for a in a_all]
        a_rb = [jnp.where(incl, a[L:, :PAIR], 0.0).astype(BF16) for a in a_all]
        av = [_dot(a_low[i], bd(probs[i][0]["v"][:, probs[i][1]])) for i in n]
        t_inv = [eye - x for x in n_mat]
        qb = [(-x).astype(BF16) for x in n_mat]
        q = [_dot(qb[i], bd(qb[i])) for i in n]
        for _ in range(int(math.log2(L)) - 2):
            qb = [x.astype(BF16) for x in q]
            tq = [_dot(jnp.concatenate([t_inv[i].astype(BF16), qb[i]], axis=0), bd(qb[i])) for i in n]
            t_inv = [t_inv[i] + tq[i][:L] for i in n]
            q = [tq[i][L:] for i in n]
        t_inv = [t_inv[i] + _dot(t_inv[i].astype(BF16), bd(q[i].astype(BF16))) for i in n]
        tx = [_dot(t_inv[i].astype(BF16),
                   jnp.concatenate([bd(probs[i][0]["kf"][:, probs[i][1]]), bd(av[i][:L].astype(BF16))], axis=1))
              for i in n]
        kft16 = [x[:, :PAIR].astype(BF16) for x in tx]
        u016 = [(-x[:, PAIR:]).astype(BF16) for x in tx]
        ry = [_dot(a_rb[i], jnp.concatenate([bd(kft16[i]), bd(u016[i])], axis=1)) for i in n]
        zeros = jnp.zeros((L, PAIR), BF16)
        mg = [_dot_tn(jnp.concatenate([jnp.concatenate([kft16[i], u016[i]], axis=1),
                                       jnp.concatenate([zeros, probs[i][0]["v"][:, probs[i][1]]], axis=1)], axis=0),
                      jnp.concatenate([probs[i][0]["bend"][:, probs[i][1]], probs[i][0]["kend"][:, probs[i][1]]],
                                      axis=0)) for i in n]
        for i in n:
            o, s = probs[i]
            p = i % N_PAIRS
            rq_ref[o["c"], :, s] = (o["r_full"][:, s] - ry[i][:, :PAIR]).astype(BF16)
            y0 = ry[i][:, PAIR:] + av[i][L:]
            y_ref[o["rs"], s] = y0 if acc_ref is None else y0 + acc_ref[o["rs"], s]
            mp_ref[o["c"], p] = jnp.where(same_head, -mg[i][:PAIR], 0.0).astype(BF16)
            g_ref[o["c"], p] = jnp.where(left_state, mg[i][PAIR:PAIR + HEAD], mg[i][PAIR + HEAD:])

    prepare()

    state = [state_ref[p] for p in pairs]
    for c in range(n_chunks):
        cc = (n_chunks - 1 - c) if reverse else c
        rs = slice(cc * L, (cc + 1) * L)
        decay = decay_ref[cc]
        sb = [x.astype(BF16) for x in state]
        carried = [_dot(sb[p], mp_ref[cc, p]) for p in pairs]
        from_state = [_dot_nt(rq_ref[cc, :, lanes[p]], _block_diag(sb[p], left_state)) for p in pairs]
        state = [state[p] * decay[:, lanes[p]] + carried[p] + g_ref[cc, p] for p in pairs]
        for p in pairs:
            y_ref[rs, lanes[p]] = y_ref[rs, lanes[p]] + from_state[p]
    for p in pairs:
        state_ref[p] = state[p]


def _scan(r, v, kk, lw, kd, b, acc, batch, seq_len, reverse):
    rows = SCAN_ROWS
    streams = (r, v, kk, lw, kd, b) + (() if acc is None else (acc,))
    nblk = seq_len // rows
    n_chunks = rows // SCAN_L

    def idx(bi, j):
        return (bi * nblk + ((nblk - 1 - j) if reverse else j), 0)

    spec = pl.BlockSpec((rows, D_MODEL), idx)
    return pl.pallas_call(
        functools.partial(_scan_kernel, rows=rows, reverse=reverse, accumulate=acc is not None),
        grid=(batch, nblk),
        in_specs=[spec] * len(streams),
        out_specs=spec,
        out_shape=jax.ShapeDtypeStruct(r.shape, F32),
        scratch_shapes=[pltpu.VMEM((N_PAIRS, HEAD, PAIR), F32),
                        pltpu.VMEM((n_chunks, SCAN_L, D_MODEL), BF16),
                        pltpu.VMEM((n_chunks, N_PAIRS, PAIR, PAIR), BF16),
                        pltpu.VMEM((n_chunks, N_PAIRS, HEAD, PAIR), F32),
                        pltpu.VMEM((n_chunks, 1, D_MODEL), F32)],
        compiler_params=_params(2),
        name="rwkv7_scan_bwd" if reverse else "rwkv7_scan_fwd",
    )(*streams)


def _row(v):
    return v.reshape(1, -1).astype(F32)


def _lora_out_padded(w2):
    z = jnp.zeros_like(w2[0])
    return jnp.stack([jnp.concatenate([w2[0], z], axis=0), jnp.concatenate([z, w2[1]], axis=0)]).astype(BF16)


def _prepare(norm_mix_g, norm_ffn_g, final_norm_g,
             sg_w_in, sg_b_in, sg_norm_g, sg_w_s, sg_b_s, sg_w_out,
             tm_mu, tm_w_r, tm_w_k, tm_w_v, tm_w0, tm_w1, tm_w2, tm_a0, tm_a1, tm_a2,
             tm_g1, tm_g2, tm_k_k, tm_k_a, tm_r_k, tm_ln_g, tm_ln_b, tm_w_o,
             ff_w_gate, ff_w_up, ff_conv_w, ff_conv_b, ff_w_down):
    head_id = jnp.arange(D_MODEL) // HEAD
    gather = (head_id[:, None] == jnp.arange(LANES)[None, :]).astype(BF16)
    scatter = gather.T
    sg, tm, ff = [], [], []
    for j in range(sg_w_in.shape[0]):
        sg.append(dict(
            w_in=sg_w_in[j].astype(BF16), b_in=_row(sg_b_in[j]), norm_g=_row(sg_norm_g[j]),
            w_s=sg_w_s[j].astype(BF16),
            b_full=jnp.repeat(sg_b_s[j].T, SG_GROUP_DIM, axis=1).astype(F32),
            w_out=sg_w_out[j].astype(BF16)))
    for j in range(tm_w_r.shape[0]):
        tm.append(dict(
            mu=tm_mu[j], w_r=tm_w_r[j].astype(BF16), w_k=tm_w_k[j].astype(BF16), w_v=tm_w_v[j].astype(BF16),
            g1=tm_g1[j].astype(BF16), g2=tm_g2[j].astype(BF16),
            w1=jnp.concatenate([tm_w1[j, 0], tm_w1[j, 1]], axis=1).astype(BF16), w2=_lora_out_padded(tm_w2[j]),
            w0=tm_w0[j],
            a1=jnp.concatenate([tm_a1[j, 0], tm_a1[j, 1]], axis=1).astype(BF16), a2=_lora_out_padded(tm_a2[j]),
            a0=tm_a0[j],
            k_k=_row(tm_k_k[j]), k_a=_row(tm_k_a[j]), r_k=_row(tm_r_k[j]),
            ln_g=_row(tm_ln_g[j]), ln_b=_row(tm_ln_b[j]), w_o=tm_w_o[j].astype(BF16), gather=gather, scatter=scatter))
    for i in range(ff_w_gate.shape[0]):
        ff.append(dict(w_gate=ff_w_gate[i].astype(BF16), w_up=ff_w_up[i].astype(BF16), conv_w=ff_conv_w[i],
                       conv_b=_row(ff_conv_b[i]), w_down=ff_w_down[i].astype(BF16)))
    return dict(mix_g=norm_mix_g, ffn_g=norm_ffn_g, final_g=_row(final_norm_g), sg=sg, tm=tm, ff=ff)


def _trunk(x3, prm):
    batch, seq_len, d_model = x3.shape
    assert d_model == D_MODEL and all(seq_len % t == 0 for t in (TM_SG, TM_FFN, TM_TM, SCAN_ROWS)), x3.shape
    x = x3.reshape(batch * seq_len, D_MODEL)
    depth = prm["mix_g"].shape[0]
    for i in range(depth):
        g = _row(prm["mix_g"][i])
        if i % 2 == 0:
            p = prm["sg"][i // 2]
            x = _sg_layer(x, g, p["w_in"], p["b_in"], p["norm_g"], p["w_s"], p["b_full"], p["w_out"])
        else:
            p = prm["tm"][i // 2]
            r, v, kk, gate, bonus, lw0, kd0, b0, lw1, kd1, b1 = _tm_pre(x, seq_len, g, p)
            y = _scan(r, v, kk, lw0, kd0, b0, None, batch, seq_len, False)
            y = _scan(r, v, kk, lw1, kd1, b1, y, batch, seq_len, True)
            rwkv_out = (y, bonus, gate, p)
        f = prm["ff"][i]
        x = _ffn_layer(x, seq_len, _row(prm["ffn_g"][i]), f["w_gate"], f["w_up"], f["conv_w"], f["conv_b"],
                       f["w_down"], prm["final_g"], final=(i == depth - 1),
                       rwkv_out=rwkv_out if i % 2 == 1 else None)
    return x.reshape(batch, seq_len, D_MODEL)


def kernel(x_prompt, x_sample, norm_mix_g, norm_ffn_g, final_norm_g, sg_w_in, sg_b_in, sg_norm_g, sg_w_s, sg_b_s, sg_w_out, tm_mu, tm_w_r, tm_w_k, tm_w_v, tm_w0, tm_w1, tm_w2, tm_a0, tm_a1, tm_a2, tm_g1, tm_g2, tm_k_k, tm_k_a, tm_r_k, tm_ln_g, tm_ln_b, tm_w_o, ff_w_gate, ff_w_up, ff_conv_w, ff_conv_b, ff_w_down):
    prm = _prepare(norm_mix_g, norm_ffn_g, final_norm_g, sg_w_in, sg_b_in, sg_norm_g, sg_w_s, sg_b_s, sg_w_out,
                   tm_mu, tm_w_r, tm_w_k, tm_w_v, tm_w0, tm_w1, tm_w2, tm_a0, tm_a1, tm_a2,
                   tm_g1, tm_g2, tm_k_k, tm_k_a, tm_r_k, tm_ln_g, tm_ln_b, tm_w_o,
                   ff_w_gate, ff_w_up, ff_conv_w, ff_conv_b, ff_w_down)
    return (_trunk(x_prompt, prm), _trunk(x_sample, prm))
```

```python
import functools
import math

import jax
import jax.numpy as jnp
from jax import lax
from jax.experimental import pallas as pl
from jax.experimental.pallas import tpu as pltpu

F32 = jnp.float32
BF16 = jnp.bfloat16

D_MODEL = 1024
SG_CHUNK = 128
SG_HALF = 2 * D_MODEL
SG_GROUPS = 8
SG_GROUP_DIM = SG_HALF // SG_GROUPS
LANES = 128
HEAD = 64
PAIR = 2 * HEAD
assert PAIR == LANES
N_PAIRS = D_MODEL // PAIR
GN_EPS = 64e-5
RMS_EPS = 1e-6
L2_EPS = 1e-12
HALO = 16
SCAN_L = 64
VMEM_LIMIT = 56 * 1024 * 1024

TM_SG = 512
TM_FFN = 512
TM_TM = 512
SCAN_ROWS = 512


def _rms(x, g):
    return x * lax.rsqrt(jnp.mean(x * x, axis=-1, keepdims=True) + RMS_EPS) * g


def _gelu(x):
    return 0.5 * x * (1.0 + lax.erf(x * (1.0 / math.sqrt(2.0))))


def _dot(a, b):
    return jnp.dot(a, b, preferred_element_type=F32)


def _dot_nt(a, b):
    return lax.dot_general(a, b, (((1,), (1,)), ((), ())), preferred_element_type=F32)


def _dot_tn(a, b):
    return lax.dot_general(a, b, (((0,), (0,)), ((), ())), preferred_element_type=F32)


def _head_sum(a, gather, scatter):
    return _dot(_dot(a.astype(BF16), gather).astype(BF16), scatter)


def _const_spec(shape):
    nd = len(shape)
    return pl.BlockSpec(shape, lambda *_: (0,) * nd, pipeline_mode=pl.Buffered(1))


def _params(n_axes=1):
    return pltpu.CompilerParams(dimension_semantics=("arbitrary",) * n_axes,
                                vmem_limit_bytes=VMEM_LIMIT)


def _halo_specs(tm, total_rows):
    per = tm // HALO
    last = total_rows // HALO - 1
    prev = pl.BlockSpec((HALO, D_MODEL), lambda i: (jnp.maximum(i * per - 1, 0), 0))
    nxt = pl.BlockSpec((HALO, D_MODEL), lambda i: (jnp.minimum((i + 1) * per, last), 0))
    return prev, nxt


def _edge_keep_mask(tm, tiles_per_seq):
    i = pl.program_id(0)
    pos = i % tiles_per_seq
    row = lax.broadcasted_iota(jnp.int32, (tm + 2 * HALO, 1), 0)
    drop = jnp.logical_or(jnp.logical_and(pos == 0, row < HALO),
                          jnp.logical_and(pos == tiles_per_seq - 1, row >= tm + HALO))
    return jnp.logical_not(drop)


def _neighbours(ext, tm):
    n = tm + 2 * HALO
    prev = pltpu.roll(ext, 1, axis=0)[HALO:HALO + tm]
    nxt = pltpu.roll(ext, n - 1, axis=0)[HALO:HALO + tm]
    return prev, ext[HALO:HALO + tm], nxt


def _sg_kernel(x_ref, g_ref, win_ref, bin_ref, ng_ref, ws_ref, bs_ref, wout_ref, o_ref, uv_ref, *, tm):
    x = x_ref[...]
    h = _rms(x, g_ref[...]).astype(BF16)
    v = _gelu(_dot(h, win_ref[:, SG_HALF:]) + bin_ref[:, SG_HALF:])
    u = _gelu(_dot(h, win_ref[:, :SG_HALF]) + bin_ref[:, :SG_HALF])
    v = _rms(v, ng_ref[...]).astype(BF16)
    for c in range(tm // SG_CHUNK):
        rows = slice(c * SG_CHUNK, (c + 1) * SG_CHUNK)
        for g in range(SG_GROUPS):
            cols = slice(g * SG_GROUP_DIM, (g + 1) * SG_GROUP_DIM)
            mixed = _dot(ws_ref[g], v[rows, cols]) + bs_ref[:, cols]
            uv_ref[rows, cols] = (u[rows, cols] * mixed).astype(BF16)
    o_ref[...] = x + _dot(uv_ref[...], wout_ref[...])


def _sg_layer(x, g, w_in, b_in, norm_g, w_s, b_full, w_out):
    rows = x.shape[0]
    tm = TM_SG
    row_spec = pl.BlockSpec((tm, D_MODEL), lambda i: (i, 0))
    return pl.pallas_call(
        functools.partial(_sg_kernel, tm=tm),
        grid=(rows // tm,),
        in_specs=[row_spec, _const_spec(g.shape), _const_spec(w_in.shape), _const_spec(b_in.shape),
                  _const_spec(norm_g.shape), _const_spec(w_s.shape), _const_spec(b_full.shape),
                  _const_spec(w_out.shape)],
        out_specs=row_spec,
        out_shape=jax.ShapeDtypeStruct(x.shape, F32),
        scratch_shapes=[pltpu.VMEM((tm, SG_HALF), BF16)],
        compiler_params=_params(),
        name="sg_mixer",
    )(x, g, w_in, b_in, norm_g, w_s, b_full, w_out)


def _rwkv_out(y, bonus, gate, lng_ref, lnb_ref, gather_ref, scatter_ref, wo_ref):
    gather, scatter = gather_ref[...], scatter_ref[...]
    d = y - _head_sum(y, gather, scatter) * (1.0 / HEAD)
    var = _head_sum(d * d, gather, scatter) * (1.0 / HEAD)
    yn = d * lax.rsqrt(var + GN_EPS) * lng_ref[...] + lnb_ref[...]
    out = ((yn + bonus.astype(F32)) * gate.astype(F32)).astype(BF16)
    return _dot(out, wo_ref[...])


def _ffn_kernel(*refs, tm, tiles_per_seq, final, fuse_rwkv_out):
    def ext(triple):
        return jnp.concatenate([t[...] for t in triple], axis=0)

    if fuse_rwkv_out:
        x3, y3, bonus3, gate3 = refs[0:3], refs[3:6], refs[6:9], refs[9:12]
        lng_ref, lnb_ref, gather_ref, scatter_ref, wo_ref = refs[12:17]
        g_ref, wg_ref, wu_ref, cw_ref, cb_ref, wd_ref, fg_ref, o_ref = refs[17:]
        xe = ext(x3) + _rwkv_out(ext(y3), ext(bonus3), ext(gate3), lng_ref, lnb_ref, gather_ref, scatter_ref, wo_ref)
    else:
        g_ref, wg_ref, wu_ref, cw_ref, cb_ref, wd_ref, fg_ref, o_ref = refs[3:]
        xe = ext(refs[0:3])
    x = xe[HALO:HALO + tm]
    he = _rms(xe, g_ref[...]).astype(BF16)
    gate = jnp.where(_edge_keep_mask(tm, tiles_per_seq), _dot(he, wg_ref[...]), 0.0)
    g_prev, g_mid, g_next = _neighbours(gate, tm)
    conv = g_prev * cw_ref[0:1, :] + g_mid * cw_ref[1:2, :] + g_next * cw_ref[2:3, :] + cb_ref[...]
    up = _dot(he[HALO:HALO + tm], wu_ref[...])
    hh = (_gelu(conv) * up).astype(BF16)
    y = x + _dot(hh, wd_ref[...])
    if final:
        y = _rms(y, fg_ref[...])
    o_ref[...] = y


def _ffn_layer(x, seq_len, g, w_gate, w_up, conv_w, conv_b, w_down, final_g, final, rwkv_out=None):
    rows = x.shape[0]
    tm = TM_FFN
    row_spec = pl.BlockSpec((tm, D_MODEL), lambda i: (i, 0))
    prev_spec, next_spec = _halo_specs(tm, rows)
    streams, consts = [x], []
    if rwkv_out:
        y, bonus, gate, p = rwkv_out
        streams += [y, bonus, gate]
        consts += [p["ln_g"], p["ln_b"], p["gather"], p["scatter"], p["w_o"]]
    consts += [g, w_gate, w_up, conv_w, conv_b, w_down, final_g]
    return pl.pallas_call(
        functools.partial(_ffn_kernel, tm=tm, tiles_per_seq=seq_len // tm, final=final,
                          fuse_rwkv_out=bool(rwkv_out)),
        grid=(rows // tm,),
        in_specs=[prev_spec, row_spec, next_spec] * len(streams) + [_const_spec(c.shape) for c in consts],
        out_specs=row_spec,
        out_shape=jax.ShapeDtypeStruct(x.shape, F32),
        compiler_params=_params(),
        name="rwkv7_out_conv_glu_ffn" if rwkv_out else "conv_glu_ffn",
    )(*[s for s in streams for _ in range(3)], *consts)


def _tm_pre_kernel(xp_ref, x_ref, xn_ref, g_ref, mu_ref, wr_ref, wk_ref, wv_ref, g1_ref, g2_ref,
                   w1_ref, w2_ref, w0_ref, a1_ref, a2_ref, a0_ref, kk_ref, ka_ref, rk_ref, gather_ref, scatter_ref,
                   r_o, v_o, kk_o, gate_o, bonus_o, lw0_o, kd0_o, b0_o, lw1_o, kd1_o, b1_o,
                   *, tm, tiles_per_seq):
    xe = jnp.concatenate([xp_ref[...], x_ref[...], xn_ref[...]], axis=0)
    he = jnp.where(_edge_keep_mask(tm, tiles_per_seq), _rms(xe, g_ref[...]), 0.0)
    h_prev, h, h_next = _neighbours(he, tm)
    xx = 0.5 * (h_prev + h_next) - h
    gather, scatter = gather_ref[...], scatter_ref[...]

    def mix(n):
        return (h + xx * mu_ref[n:n + 1, :]).astype(BF16)

    k = _dot(mix(2), wk_ref[...])
    w_lora = jnp.tanh(_dot(mix(1), w1_ref[...])).astype(BF16)
    a_lora = _dot(mix(4), a1_ref[...]).astype(BF16)
    kk_raw = k * kk_ref[...]
    kk = kk_raw / jnp.maximum(jnp.sqrt(_head_sum(kk_raw * kk_raw, gather, scatter)), L2_EPS)
    kk_o[...] = kk.astype(BF16)

    def direction(e, lw_o, kd_o, b_o):
        w_pre = _dot(w_lora, w2_ref[e]) + w0_ref[e:e + 1, :]
        lw_o[...] = -math.exp(-0.5) * jax.nn.sigmoid(w_pre)
        a = jax.nn.sigmoid(_dot(a_lora, a2_ref[e]) + a0_ref[e:e + 1, :])
        kd = k * (1.0 + (a - 1.0) * ka_ref[...])
        kd_o[...] = kd.astype(BF16)
        b_o[...] = (kk * a).astype(BF16)
        return kd

    r = _dot(mix(0), wr_ref[...])
    kd_fwd = direction(0, lw0_o, kd0_o, b0_o)
    v = _dot(mix(3), wv_ref[...])
    kd_bwd = direction(1, lw1_o, kd1_o, b1_o)
    gate_o[...] = _dot(jax.nn.sigmoid(_dot(mix(5), g1_ref[...])).astype(BF16), g2_ref[...]).astype(BF16)
    r_o[...] = r.astype(BF16)
    v_o[...] = v.astype(BF16)
    bonus_o[...] = (_head_sum(r * (kd_fwd + kd_bwd) * rk_ref[...], gather, scatter) * v).astype(BF16)


def _tm_pre(x, seq_len, g, p):
    rows = x.shape[0]
    tm = TM_TM
    row_spec = pl.BlockSpec((tm, D_MODEL), lambda i: (i, 0))
    prev_spec, next_spec = _halo_specs(tm, rows)
    consts = (g, p["mu"], p["w_r"], p["w_k"], p["w_v"], p["g1"], p["g2"], p["w1"], p["w2"], p["w0"],
              p["a1"], p["a2"], p["a0"], p["k_k"], p["k_a"], p["r_k"], p["gather"], p["scatter"])
    half = jax.ShapeDtypeStruct(x.shape, BF16)
    full = jax.ShapeDtypeStruct(x.shape, F32)
    return pl.pallas_call(
        functools.partial(_tm_pre_kernel, tm=tm, tiles_per_seq=seq_len // tm),
        grid=(rows // tm,),
        in_specs=[prev_spec, row_spec, next_spec] + [_const_spec(c.shape) for c in consts],
        out_specs=[row_spec] * 11,
        out_shape=[half, half, half, half, half, full, half, half, full, half, half],
        compiler_params=_params(),
        name="rwkv7_projections",
    )(x, x, x, *consts)


def _block_diag(x, left):
    zero = jnp.zeros_like(x)
    return jnp.concatenate([jnp.where(left, x, zero), jnp.where(left, zero, x)], axis=0)


def _scan_kernel(*refs, rows, reverse, accumulate):
    r_ref, v_ref, kk_ref, lw_ref, kd_ref, b_ref = refs[:6]
    acc_ref = refs[6] if accumulate else None
    y_ref, state_ref, rq_ref, mp_ref, g_ref, decay_ref = refs[-6:]
    _scan_body(r_ref, v_ref, kk_ref, lw_ref, kd_ref, b_ref, acc_ref, y_ref,
               state_ref, rq_ref, mp_ref, g_ref, decay_ref, rows=rows, reverse=reverse)


def _scan_body(r_ref, v_ref, kk_ref, lw_ref, kd_ref, b_ref, acc_ref, y_ref,
               state_ref, rq_ref, mp_ref, g_ref, decay_ref, *, rows, reverse):
    L = SCAN_L
    n_chunks = rows // L
    pairs = range(N_PAIRS)
    lanes = [slice(p * PAIR, (p + 1) * PAIR) for p in pairs]

    @pl.when(pl.program_id(1) == 0)
    def _():
        state_ref[...] = jnp.zeros_like(state_ref)

    t_idx = lax.broadcasted_iota(jnp.int32, (L, PAIR), 0)
    lane = lax.broadcasted_iota(jnp.int32, (L, PAIR), 1)
    s_idx = lane % L
    left = lane < HEAD
    if reverse:
        strict, incl = s_idx > t_idx, s_idx >= t_idx
    else:
        strict, incl = s_idx < t_idx, s_idx <= t_idx
    eye = jnp.where(s_idx == t_idx, 1.0, 0.0)
    left_state = lax.broadcasted_iota(jnp.int32, (HEAD, PAIR), 1) < HEAD
    sq_row = lax.broadcasted_iota(jnp.int32, (PAIR, PAIR), 0) < HEAD
    sq_col = lax.broadcasted_iota(jnp.int32, (PAIR, PAIR), 1) < HEAD
    same_head = sq_row == sq_col
    t_full = lax.broadcasted_iota(jnp.int32, (L, D_MODEL), 0)
    last = 0 if reverse else L - 1

    def bd(x):
        return _block_diag(x, left)

    def chunk_operands(c):
        rs = slice(c * L, (c + 1) * L)
        lw = lw_ref[rs, :]
        cum = lw
        for s in (1 << i for i in range(int(math.log2(L)))):
            if reverse:
                cum = cum + jnp.where(t_full < L - s, pltpu.roll(cum, L - s, axis=0), 0.0)
            else:
                cum = cum + jnp.where(t_full >= s, pltpu.roll(cum, s, axis=0), 0.0)
        half = 0.5 * cum[last:last + 1, :]
        e_half = jnp.exp(half)
        decay_ref[c] = e_half * e_half
        q_kappa = kk_ref[rs, :].astype(F32) * jnp.exp(cum - lw - half)
        q_r = r_ref[rs, :].astype(F32) * jnp.exp(cum - half)
        e_neg = jnp.exp(half - cum)
        b_t = b_ref[rs, :].astype(F32) * e_neg
        k_t = kd_ref[rs, :].astype(F32) * e_neg
        return dict(
            rs=rs, c=c,
            q_in=jnp.concatenate([q_kappa, q_r], axis=0).astype(BF16),
            bt=b_t.astype(BF16), kt=k_t.astype(BF16), v=v_ref[rs, :],
            kf=(q_kappa * e_half).astype(BF16),
            r_full=q_r * e_half,
            bend=(b_t * e_half).astype(BF16),
            kend=(k_t * e_half).astype(BF16))

    def prepare():
        ops = [chunk_operands(c) for c in range(n_chunks)]
        probs = [(o, s) for o in ops for s in lanes]
        n = range(len(probs))
        a_all = [_dot_nt(o["q_in"][:, s], jnp.concatenate([bd(o["bt"][:, s]), bd(o["kt"][:, s])], axis=0))
                 for o, s in probs]
        n_mat = [jnp.where(strict, a[:L, :PAIR], 0.0) for a in a_all]
        a_low = [jnp.concatenate([jnp.where(strict, a[:L, PAIR:], 0.0), jnp.where(incl, a[L:, PAIR:], 0.0)],
                                 axis=0).astype(BF16) for a in a_all]
        a_rb = [jnp.where(incl, a[L:, :PAIR], 0.0).astype(BF16) for a in a_all]
        av = [_dot(a_low[i], bd(probs[i][0]["v"][:, probs[i][1]])) for i in n]
        t_inv = [eye - x for x in n_mat]
        qb = [(-x).astype(BF16) for x in n_mat]
        q = [_dot(qb[i], bd(qb[i])) for i in n]
        for _ in range(int(math.log2(L)) - 2):
            qb = [x.astype(BF16) for x in q]
            tq = [_dot(jnp.concatenate([t_inv[i].astype(BF16), qb[i]], axis=0), bd(qb[i])) for i in n]
            t_inv = [t_inv[i] + tq[i][:L] for i in n]
            q = [tq[i][L:] for i in n]
        t_inv = [t_inv[i] + _dot(t_inv[i].astype(BF16), bd(q[i].astype(BF16))) for i in n]
        tx = [_dot(t_inv[i].astype(BF16),
                   jnp.concatenate([bd(probs[i][0]["kf"][:, probs[i][1]]), bd(av[i][:L].astype(BF16))], axis=1))
              for i in n]
        kft16 = [x[:, :PAIR].astype(BF16) for x in tx]
        u016 = [(-x[:, PAIR:]).astype(BF16) for x in tx]
        ry = [_dot(a_rb[i], jnp.concatenate([bd(kft16[i]), bd(u016[i])], axis=1)) for i in n]
        zeros = jnp.zeros((L, PAIR), BF16)
        mg = [_dot_tn(jnp.concatenate([jnp.concatenate([kft16[i], u016[i]], axis=1),
                                       jnp.concatenate([zeros, probs[i][0]["v"][:, probs[i][1]]], axis=1)], axis=0),
                      jnp.concatenate([probs[i][0]["bend"][:, probs[i][1]], probs[i][0]["kend"][:, probs[i][1]]],
                                      axis=0)) for i in n]
        for i in n:
            o, s = probs[i]
            p = i % N_PAIRS
            rq_ref[o["c"], :, s] = (o["r_full"][:, s] - ry[i][:, :PAIR]).astype(BF16)
            y0 = ry[i][:, PAIR:] + av[i][L:]
            y_ref[o["rs"], s] = y0 if acc_ref is None else y0 + acc_ref[o["rs"], s]
            mp_ref[o["c"], p] = jnp.where(same_head, -mg[i][:PAIR], 0.0).astype(BF16)
            g_ref[o["c"], p] = jnp.where(left_state, mg[i][PAIR:PAIR + HEAD], mg[i][PAIR + HEAD:])

    prepare()

    state = [state_ref[p] for p in pairs]
    for c in range(n_chunks):
        cc = (n_chunks - 1 - c) if reverse else c
        rs = slice(cc * L, (cc + 1) * L)
        decay = decay_ref[cc]
        sb = [x.astype(BF16) for x in state]
        carried = [_dot(sb[p], mp_ref[cc, p]) for p in pairs]
        from_state = [_dot_nt(rq_ref[cc, :, lanes[p]], _block_diag(sb[p], left_state)) for p in pairs]
        state = [state[p] * decay[:, lanes[p]] + carried[p] + g_ref[cc, p] for p in pairs]
        for p in pairs:
            y_ref[rs, lanes[p]] = y_ref[rs, lanes[p]] + from_state[p]
    for p in pairs:
        state_ref[p] = state[p]


def _scan(r, v, kk, lw, kd, b, acc, batch, seq_len, reverse):
    rows = SCAN_ROWS
    streams = (r, v, kk, lw, kd, b) + (() if acc is None else (acc,))
    nblk = seq_len // rows
    n_chunks = rows // SCAN_L

    def idx(bi, j):
        return (bi * nblk + ((nblk - 1 - j) if reverse else j), 0)

    spec = pl.BlockSpec((rows, D_MODEL), idx)
    return pl.pallas_call(
        functools.partial(_scan_kernel, rows=rows, reverse=reverse, accumulate=acc is not None),
        grid=(batch, nblk),
        in_specs=[spec] * len(streams),
        out_specs=spec,
        out_shape=jax.ShapeDtypeStruct(r.shape, F32),
        scratch_shapes=[pltpu.VMEM((N_PAIRS, HEAD, PAIR), F32),
                        pltpu.VMEM((n_chunks, SCAN_L, D_MODEL), BF16),
                        pltpu.VMEM((n_chunks, N_PAIRS, PAIR, PAIR), BF16),
                        pltpu.VMEM((n_chunks, N_PAIRS, HEAD, PAIR), F32),
                        pltpu.VMEM((n_chunks, 1, D_MODEL), F32)],
        compiler_params=_params(2),
        name="rwkv7_scan_bwd" if reverse else "rwkv7_scan_fwd",
    )(*streams)


def _row(v):
    return v.reshape(1, -1).astype(F32)


def _lora_out_padded(w2):
    z = jnp.zeros_like(w2[0])
    return jnp.stack([jnp.concatenate([w2[0], z], axis=0), jnp.concatenate([z, w2[1]], axis=0)]).astype(BF16)


def _prepare(norm_mix_g, norm_ffn_g, final_norm_g,
             sg_w_in, sg_b_in, sg_norm_g, sg_w_s, sg_b_s, sg_w_out,
             tm_mu, tm_w_r, tm_w_k, tm_w_v, tm_w0, tm_w1, tm_w2, tm_a0, tm_a1, tm_a2,
             tm_g1, tm_g2, tm_k_k, tm_k_a, tm_r_k, tm_ln_g, tm_ln_b, tm_w_o,
             ff_w_gate, ff_w_up, ff_conv_w, ff_conv_b, ff_w_down):
    head_id = jnp.arange(D_MODEL) // HEAD
    gather = (head_id[:, None] == jnp.arange(LANES)[None, :]).astype(BF16)
    scatter = gather.T
    sg, tm, ff = [], [], []
    for j in range(sg_w_in.shape[0]):
        sg.append(dict(
            w_in=sg_w_in[j].astype(BF16), b_in=_row(sg_b_in[j]), norm_g=_row(sg_norm_g[j]),
            w_s=sg_w_s[j].astype(BF16),
            b_full=jnp.repeat(sg_b_s[j].T, SG_GROUP_DIM, axis=1).astype(F32),
            w_out=sg_w_out[j].astype(BF16)))
    for j in range(tm_w_r.shape[0]):
        tm.append(dict(
            mu=tm_mu[j], w_r=tm_w_r[j].astype(BF16), w_k=tm_w_k[j].astype(BF16), w_v=tm_w_v[j].astype(BF16),
            g1=tm_g1[j].astype(BF16), g2=tm_g2[j].astype(BF16),
            w1=jnp.concatenate([tm_w1[j, 0], tm_w1[j, 1]], axis=1).astype(BF16), w2=_lora_out_padded(tm_w2[j]),
            w0=tm_w0[j],
            a1=jnp.concatenate([tm_a1[j, 0], tm_a1[j, 1]], axis=1).astype(BF16), a2=_lora_out_padded(tm_a2[j]),
            a0=tm_a0[j],
            k_k=_row(tm_k_k[j]), k_a=_row(tm_k_a[j]), r_k=_row(tm_r_k[j]),
            ln_g=_row(tm_ln_g[j]), ln_b=_row(tm_ln_b[j]), w_o=tm_w_o[j].astype(BF16), gather=gather, scatter=scatter))
    for i in range(ff_w_gate.shape[0]):
        ff.append(dict(w_gate=ff_w_gate[i].astype(BF16), w_up=ff_w_up[i].astype(BF16), conv_w=ff_conv_w[i],
                       conv_b=_row(ff_conv_b[i]), w_down=ff_w_down[i].astype(BF16)))
    return dict(mix_g=norm_mix_g, ffn_g=norm_ffn_g, final_g=_row(final_norm_g), sg=sg, tm=tm, ff=ff)


def _trunk(x3, prm):
    batch, seq_len, d_model = x3.shape
    assert d_model == D_MODEL and all(seq_len % t == 0 for t in (TM_SG, TM_FFN, TM_TM, SCAN_ROWS)), x3.shape
    x = x3.reshape(batch * seq_len, D_MODEL)
    depth = prm["mix_g"].shape[0]
    for i in range(depth):
        g = _row(prm["mix_g"][i])
        if i % 2 == 0:
            p = prm["sg"][i // 2]
            x = _sg_layer(x, g, p["w_in"], p["b_in"], p["norm_g"], p["w_s"], p["b_full"], p["w_out"])
        else:
            p = prm["tm"][i // 2]
            r, v, kk, gate, bonus, lw0, kd0, b0, lw1, kd1, b1 = _tm_pre(x, seq_len, g, p)
            y = _scan(r, v, kk, lw0, kd0, b0, None, batch, seq_len, False)
            y = _scan(r, v, kk, lw1, kd1, b1, y, batch, seq_len, True)
            rwkv_out = (y, bonus, gate, p)
        f = prm["ff"][i]
        x = _ffn_layer(x, seq_len, _row(prm["ffn_g"][i]), f["w_gate"], f["w_up"], f["conv_w"], f["conv_b"],
                       f["w_down"], prm["final_g"], final=(i == depth - 1),
                       rwkv_out=rwkv_out if i % 2 == 1 else None)
    return x.reshape(batch, seq_len, D_MODEL)


def kernel(x_prompt, x_sample, norm_mix_g, norm_ffn_g, final_norm_g, sg_w_in, sg_b_in, sg_norm_g, sg_w_s, sg_b_s, sg_w_out, tm_mu, tm_w_r, tm_w_k, tm_w_v, tm_w0, tm_w1, tm_w2, tm_a0, tm_a1, tm_a2, tm_g1, tm_g2, tm_k_k, tm_k_a, tm_r_k, tm_ln_g, tm_ln_b, tm_w_o, ff_w_gate, ff_w_up, ff_conv_w, ff_conv_b, ff_w_down):
    prm = _prepare(norm_mix_g, norm_ffn_g, final_norm_g, sg_w_in, sg_b_in, sg_norm_g, sg_w_s, sg_b_s, sg_w_out,
                   tm_mu, tm_w_r, tm_w_k, tm_w_v, tm_w0, tm_w1, tm_w2, tm_a0, tm_a1, tm_a2,
                   tm_g1, tm_g2, tm_k_k, tm_k_a, tm_r_k, tm_ln_g, tm_ln_b, tm_w_o,
                   ff_w_gate, ff_w_up, ff_conv_w, ff_conv_b, ff_w_down)
    return (_trunk(x_prompt, prm), _trunk(x_sample, prm))
```
